```python
import math
import functools
import jax
import jax.numpy as jnp
from jax import lax
import numpy as np

D_MODEL = 2048
BATCH = 2
SEQ = 4096
DEPTH = 4
DEC_BATCH = 8
DEC_SEQ = 8
PAST_LEN = 16384
PAGE_SIZE = 128

N_META = 16
H_A = 16
DH_A = 64
W_A = H_A * 2 * DH_A
H_M = 8
DK_M = 128
DV_M = 256
W_QK_M = H_M * DK_M
W_M = H_M * DV_M
CONV_W = 4
CHUNK = 64
Q_BLOCK = 128
EPS = 1e-6

kernel_name = 'hybrid_diffattn_mlstm_step'


def rmsnorm(x, g):
    xf = x.astype(jnp.float32)
    y = xf * lax.rsqrt(jnp.mean(xf * xf, axis=-1, keepdims=True) + EPS)
    return (y * g.astype(jnp.float32)).astype(x.dtype)


def split_in(u):
    sizes = (W_A, W_A, W_A, W_A, W_QK_M, W_QK_M, W_M, W_M, W_M, H_M, H_M, D_MODEL, D_MODEL)
    return jnp.split(u, np.cumsum(sizes)[:-1].tolist(), axis=-1)


def diff_lambda(l, lam_q1, lam_k1, lam_q2, lam_k2):
    lam_init = 0.8 - 0.6 * math.exp(-0.3 * l)
    def e(a, b):
        return jnp.exp(jnp.sum(a[l].astype(jnp.float32) * b[l].astype(jnp.float32)))
    return e(lam_q1, lam_k1) - e(lam_q2, lam_k2) + lam_init, lam_init


def diff_softmax_attend(q1, q2, k1, k2, v, mask, lam):
    scale = DH_A ** -0.5
    def probs(qa, ka):
        s = jnp.einsum('bqhd,bkhd->bhqk', qa, ka).astype(jnp.float32) * scale
        return jax.nn.softmax(jnp.where(mask, s, -jnp.inf), axis=-1)
    p = probs(q1, k1) - lam * probs(q2, k2)
    return jnp.einsum('bhqk,bkhd->bqhd', p.astype(v.dtype), v)


def prompt_attention(q, k, v, lam):
    bsz, length = q.shape[0], q.shape[1]
    n_real = length - N_META
    q1, q2 = q[..., :DH_A], q[..., DH_A:]
    k1, k2 = k[..., :DH_A], k[..., DH_A:]
    meta_mask = jnp.tril(jnp.ones((N_META, N_META), bool))
    o_meta = diff_softmax_attend(q1[:, :N_META], q2[:, :N_META], k1[:, :N_META], k2[:, :N_META],
                                 v[:, :N_META], meta_mask, lam)
    key_pos = jnp.arange(length)
    def block(i):
        start = N_META + i * Q_BLOCK
        qpos = start + jnp.arange(Q_BLOCK)
        mask = key_pos[None, :] <= qpos[:, None]
        qb1 = lax.dynamic_slice_in_dim(q1, start, Q_BLOCK, axis=1)
        qb2 = lax.dynamic_slice_in_dim(q2, start, Q_BLOCK, axis=1)
        return diff_softmax_attend(qb1, qb2, k1, k2, v, mask, lam)
    o_real = lax.map(block, jnp.arange(n_real // Q_BLOCK))
    o_real = jnp.moveaxis(o_real, 0, 1).reshape(bsz, n_real, H_A, 2 * DH_A)
    return jnp.concatenate([o_meta, o_real], axis=1)


def sample_attention(q, k, v, lam, k_past, v_past):
    t = q.shape[1]
    p_len = k_past.shape[1]
    scale = DH_A ** -0.5
    new_mask = jnp.tril(jnp.ones((t, t), bool))
    def probs(qa, kpa, kna):
        sp = jnp.einsum('bqhd,bkhd->bhqk', qa, kpa).astype(jnp.float32) * scale
        sn = jnp.einsum('bqhd,bkhd->bhqk', qa, kna).astype(jnp.float32) * scale
        sn = jnp.where(new_mask, sn, -jnp.inf)
        return jax.nn.softmax(jnp.concatenate([sp, sn], axis=-1), axis=-1)
    p = probs(q[..., :DH_A], k_past[..., :DH_A], k[..., :DH_A]) \
        - lam * probs(q[..., DH_A:], k_past[..., DH_A:], k[..., DH_A:])
    p = p.astype(v.dtype)
    return (jnp.einsum('bhqk,bkhd->bqhd', p[..., :p_len], v_past)
            + jnp.einsum('bhqk,bkhd->bqhd', p[..., p_len:], v))


def causal_conv(u, w, b, prev):
    t = u.shape[1]
    full = jnp.concatenate([prev.astype(u.dtype), u], axis=1)
    out = b + full[:, 0:t] * w[0]
    for j in range(1, CONV_W):
        out = out + full[:, j:j + t] * w[j]
    return out, full[:, full.shape[1] - (CONV_W - 1):]


def mlstm_run(q, k, v, log_i, log_f, state, chunk):
    bsz, t = q.shape[0], q.shape[1]
    nc = t // chunk
    def to_chunks(a):
        a = a.astype(jnp.float32).reshape((bsz, nc, chunk) + a.shape[2:])
        return jnp.moveaxis(a, (1, 2), (0, 3))
    xs = (to_chunks(q), to_chunks(k), to_chunks(v), to_chunks(log_i), to_chunks(log_f))
    tril = jnp.tril(jnp.ones((chunk, chunk), bool))
    def step(carry, xc):
        c_mat, n_vec, m = carry
        qc, kc, vc, lic, lfc = xc
        b = jnp.cumsum(lfc, axis=-1)
        a = b + m[..., None]
        d = b[..., :, None] - b[..., None, :] + lic[..., None, :]
        d = jnp.where(tril, d, -jnp.inf)
        mt = jnp.maximum(a, jnp.max(d, axis=-1))
        w_prev = jnp.exp(a - mt)
        s = jnp.einsum('bhtd,bhsd->bhts', qc, kc) * jnp.exp(d - mt[..., None])
        num = (jnp.einsum('bhts,bhsv->bhtv', s, vc)
               + w_prev[..., None] * jnp.einsum('bhvd,bhtd->bhtv', c_mat, qc))
        den = jnp.sum(s, axis=-1) + w_prev * jnp.einsum('bhd,bhtd->bht', n_vec, qc)
        h = num / jnp.maximum(jnp.abs(den), jnp.exp(-mt))[..., None]
        a_end = b[..., -1] + m
        g = b[..., -1:] - b + lic
        m_new = jnp.maximum(a_end, jnp.max(g, axis=-1))
        w_old = jnp.exp(a_end - m_new)
        w_tok = jnp.exp(g - m_new[..., None])
        c_new = w_old[..., None, None] * c_mat + jnp.einsum('bhs,bhsv,bhsd->bhvd', w_tok, vc, kc)
        n_new = w_old[..., None] * n_vec + jnp.einsum('bhs,bhsd->bhd', w_tok, kc)
        return (c_new, n_new, m_new), h
    state = tuple(s_.astype(jnp.float32) for s_ in state)
    state, h = lax.scan(step, state, xs)
    h = jnp.moveaxis(h, (0, 3), (1, 2)).reshape(bsz, t, H_M, DV_M)
    return h, state


def prompt_recurrence(q, k, v, log_i, log_f):
    bsz = q.shape[0]
    state = (jnp.zeros((bsz, H_M, DV_M, DK_M), jnp.float32),
             jnp.zeros((bsz, H_M, DK_M), jnp.float32),
             jnp.zeros((bsz, H_M), jnp.float32))
    h_meta, state = mlstm_run(q[:, :N_META], k[:, :N_META], v[:, :N_META],
                              log_i[:, :N_META], log_f[:, :N_META], state, N_META)
    h_real, state = mlstm_run(q[:, N_META:], k[:, N_META:], v[:, N_META:],
                              log_i[:, N_META:], log_f[:, N_META:], state, CHUNK)
    return jnp.concatenate([h_meta, h_real], axis=1), state


def mixer_layer(x, l, attend, recur, conv_prev, norm_g, w_in, b_i, b_f, conv_w, conv_b,
                lam_q1, lam_k1, lam_q2, lam_k2, subln_g, head_g, w_pa, w_pm, w_out):
    bsz, t, _ = x.shape
    xn = rmsnorm(x, norm_g[l])
    a_q, a_k, a_v, a_z, m_q, m_k, m_v, m_o, m_z, m_i, m_f, g_a, g_m = split_in(xn @ w_in[l])
    lam, lam_init = diff_lambda(l, lam_q1, lam_k1, lam_q2, lam_k2)
    q = a_q.reshape(bsz, t, H_A, 2 * DH_A)
    k = a_k.reshape(bsz, t, H_A, 2 * DH_A)
    v = a_v.reshape(bsz, t, H_A, 2 * DH_A)
    att = rmsnorm(attend(q, k, v, lam), subln_g[l]) * (1.0 - lam_init)
    y_a = (att.reshape(bsz, t, W_A) * jax.nn.silu(a_z)) @ w_pa[l]
    qk, conv_new = causal_conv(jnp.concatenate([m_q, m_k], axis=-1), conv_w[l], conv_b[l], conv_prev)
    qk = jax.nn.silu(qk)
    mq = qk[..., :W_QK_M].reshape(bsz, t, H_M, DK_M) * (DK_M ** -0.5)
    mk = qk[..., W_QK_M:].reshape(bsz, t, H_M, DK_M)
    mv = m_v.reshape(bsz, t, H_M, DV_M)
    log_i = (m_i + b_i[l]).astype(jnp.float32)
    log_f = jax.nn.log_sigmoid((m_f + b_f[l]).astype(jnp.float32))
    hm, rec_state = recur(mq, mk, mv, log_i, log_f)
    hm = rmsnorm(hm.astype(x.dtype), head_g[l].reshape(H_M, DV_M)).reshape(bsz, t, W_M)
    y_m = (hm * jax.nn.sigmoid(m_o) * jax.nn.silu(m_z)) @ w_pm[l]
    merged = jax.nn.sigmoid(g_a) * y_a + jax.nn.sigmoid(g_m) * y_m
    return x + merged @ w_out[l], k, v, conv_new, rec_state


def setup_inputs(seed: int = 0) -> dict:
    key = jax.random.key(seed)
    ks = jax.random.split(key, 26)
    n_pages = PAST_LEN // PAGE_SIZE
    n_phys = (DEC_BATCH * n_pages * 5) // 4
    n_in = 4 * W_A + 2 * W_QK_M + 3 * W_M + 2 * H_M + 2 * D_MODEL
    def nrm(k, shape, s):
        return jax.random.normal(k, shape, jnp.float32) * s
    page_table = jax.random.permutation(ks[4], n_phys)[:DEC_BATCH * n_pages]
    page_table = page_table.reshape(DEC_BATCH, n_pages).astype(jnp.int32)
    return {
        'x_prompt': nrm(ks[0], (BATCH, SEQ, D_MODEL), 1.0),
        'x_sample': nrm(ks[1], (DEC_BATCH, DEC_SEQ, D_MODEL), 1.0),
        'cache_k': nrm(ks[2], (DEPTH, n_phys, PAGE_SIZE, H_A, 2 * DH_A), 1.0),
        'cache_v': nrm(ks[3], (DEPTH, n_phys, PAGE_SIZE, H_A, 2 * DH_A), 1.0),
        'page_table': page_table,
        'state_conv': nrm(ks[5], (DEPTH, DEC_BATCH, CONV_W - 1, 2 * W_QK_M), 1.0),
        'state_C': nrm(ks[6], (DEPTH, DEC_BATCH, H_M, DV_M, DK_M), DK_M ** -0.5),
        'state_n': nrm(ks[7], (DEPTH, DEC_BATCH, H_M, DK_M), DK_M ** -0.5),
        'state_m': nrm(ks[8], (DEPTH, DEC_BATCH, H_M), 1.0),
        'meta': nrm(ks[9], (N_META, D_MODEL), 1.0),
        'norm_g': 1.0 + nrm(ks[10], (DEPTH, D_MODEL), 0.02),
        'w_in': nrm(ks[11], (DEPTH, D_MODEL, n_in), D_MODEL ** -0.5),
        'b_i': nrm(ks[12], (DEPTH, H_M), 0.1),
        'b_f': jnp.linspace(3.0, 6.0, H_M, dtype=jnp.float32)[None] + nrm(ks[13], (DEPTH, H_M), 0.1),
        'conv_w': nrm(ks[14], (DEPTH, CONV_W, 2 * W_QK_M), CONV_W ** -0.5),
        'conv_b': nrm(ks[15], (DEPTH, 2 * W_QK_M), 0.01),
        'lam_q1': nrm(ks[16], (DEPTH, DH_A), 0.1),
        'lam_k1': nrm(ks[17], (DEPTH, DH_A), 0.1),
        'lam_q2': nrm(ks[18], (DEPTH, DH_A), 0.1),
        'lam_k2': nrm(ks[19], (DEPTH, DH_A), 0.1),
        'subln_g': 1.0 + nrm(ks[20], (DEPTH, 2 * DH_A), 0.02),
        'head_g': 1.0 + nrm(ks[21], (DEPTH, W_M), 0.02),
        'w_pa': nrm(ks[22], (DEPTH, W_A, D_MODEL), W_A ** -0.5),
        'w_pm': nrm(ks[23], (DEPTH, W_M, D_MODEL), W_M ** -0.5),
        'w_out': nrm(ks[24], (DEPTH, D_MODEL, D_MODEL), D_MODEL ** -0.5),
        'norm_f': 1.0 + nrm(ks[25], (D_MODEL,), 0.02),
    }


def reference(x_prompt, x_sample, cache_k, cache_v, page_table, state_conv, state_C, state_n, state_m,
              meta, norm_g, w_in, b_i, b_f, conv_w, conv_b, lam_q1, lam_k1, lam_q2, lam_k2,
              subln_g, head_g, w_pa, w_pm, w_out, norm_f):
    weights = (norm_g, w_in, b_i, b_f, conv_w, conv_b, lam_q1, lam_k1, lam_q2, lam_k2,
               subln_g, head_g, w_pa, w_pm, w_out)
    bsz = x_prompt.shape[0]
    h = jnp.concatenate([jnp.broadcast_to(meta[None].astype(x_prompt.dtype), (bsz, N_META, D_MODEL)),
                         x_prompt], axis=1)
    conv0 = jnp.zeros((bsz, CONV_W - 1, 2 * W_QK_M), x_prompt.dtype)
    kp_l, vp_l, cp_l, Cp_l, np_l, mp_l = [], [], [], [], [], []
    for l in range(DEPTH):
        h, k, v, cv, (c_s, n_s, m_s) = mixer_layer(h, l, prompt_attention, prompt_recurrence, conv0, *weights)
        kp_l.append(k); vp_l.append(v); cp_l.append(cv)
        Cp_l.append(c_s.astype(x_prompt.dtype)); np_l.append(n_s.astype(x_prompt.dtype)); mp_l.append(m_s.astype(x_prompt.dtype))
    y_prompt = rmsnorm(h, norm_f)[:, N_META:]
    dbsz, t_new = x_sample.shape[0], x_sample.shape[1]
    h = x_sample
    ks_l, vs_l, cs_l, Cs_l, ns_l, ms_l = [], [], [], [], [], []
    for l in range(DEPTH):
        k_past = cache_k[l][page_table].reshape(dbsz, -1, H_A, 2 * DH_A)
        v_past = cache_v[l][page_table].reshape(dbsz, -1, H_A, 2 * DH_A)
        attend = functools.partial(sample_attention, k_past=k_past, v_past=v_past)
        recur = functools.partial(mlstm_run, state=(state_C[l], state_n[l], state_m[l]), chunk=t_new)
        h, k, v, cv, (c_s, n_s, m_s) = mixer_layer(h, l, attend, recur, state_conv[l], *weights)
        ks_l.append(k); vs_l.append(v); cs_l.append(cv)
        Cs_l.append(c_s.astype(x_sample.dtype)); ns_l.append(n_s.astype(x_sample.dtype)); ms_l.append(m_s.astype(x_sample.dtype))
    y_sample = rmsnorm(h, norm_f)
    return (y_prompt, y_sample,
            jnp.stack(kp_l), jnp.stack(vp_l), jnp.stack(ks_l), jnp.stack(vs_l),
            jnp.stack(cp_l), jnp.stack(cs_l),
            jnp.stack(Cp_l), jnp.stack(np_l), jnp.stack(mp_l),
            jnp.stack(Cs_l), jnp.stack(ns_l), jnp.stack(ms_l))
```

```python
import functools
import math

import jax
import jax.numpy as jnp
from jax import lax
from jax.experimental import pallas as pl
from jax.experimental.pallas import tpu as pltpu

f32 = jnp.float32
bf16 = jnp.bfloat16

EPS = 1e-6
LANES = 128
V7X_VMEM_LIMIT = 56 * 1024 * 1024
NT = (((1,), (1,)), ((), ()))
TN = (((0,), (0,)), ((), ()))


def _params(*sem):
    return pltpu.CompilerParams(dimension_semantics=sem, vmem_limit_bytes=V7X_VMEM_LIMIT)


def _mm_kernel(x_ref, w_ref, *o_refs):
    acc = jnp.dot(x_ref[...].astype(bf16), w_ref[...], preferred_element_type=f32)
    for o_ref in o_refs:
        o_ref[...] = acc.astype(o_ref.dtype)


def _matmul(x, w, layer, col0, ncols, out_dtypes, tm, tn):
    m, k = x.shape
    tm = min(tm, m)
    tn = min(tn, ncols)
    assert col0 % tn == 0 and ncols % tn == 0
    return pl.pallas_call(
        _mm_kernel,
        grid=(ncols // tn, pl.cdiv(m, tm)),
        in_specs=[pl.BlockSpec((tm, k), lambda j, i: (i, 0)),
                  pl.BlockSpec((None, k, tn), lambda j, i: (layer, 0, col0 // tn + j))],
        out_specs=[pl.BlockSpec((tm, tn), lambda j, i: (i, j)) for _ in out_dtypes],
        out_shape=[jax.ShapeDtypeStruct((m, ncols), dt) for dt in out_dtypes],
        compiler_params=_params("arbitrary", "arbitrary"),
    )(x, w)


def _out_kernel(a_ref, m_ref, ga_ref, gm_ref, x_ref, wpa_ref, wpm_ref, wo_ref, o_ref):
    ya = jnp.dot(a_ref[...].astype(bf16), wpa_ref[...], preferred_element_type=f32)
    ym = jnp.dot(m_ref[...].astype(bf16), wpm_ref[...], preferred_element_type=f32)
    merged = (jax.nn.sigmoid(ga_ref[...].astype(f32)) * ya
              + jax.nn.sigmoid(gm_ref[...].astype(f32)) * ym)
    o_ref[...] = x_ref[...] + jnp.dot(merged.astype(bf16), wo_ref[...], preferred_element_type=f32)


def _out_proj(a, mm, u5, ga_blk, gm_blk, x, w_pa, w_pm, w_out, layer, tm):
    m, d = x.shape
    tm = min(tm, m)
    wspec = pl.BlockSpec((None, d, d), lambda i: (layer, 0, 0), pipeline_mode=pl.Buffered(1))
    row = pl.BlockSpec((tm, d), lambda i: (i, 0))
    return pl.pallas_call(
        _out_kernel,
        grid=(pl.cdiv(m, tm),),
        in_specs=[row, row,
                  pl.BlockSpec((tm, d), lambda i: (i, ga_blk)),
                  pl.BlockSpec((tm, d), lambda i: (i, gm_blk)),
                  row, wspec, wspec, wspec],
        out_specs=row,
        out_shape=jax.ShapeDtypeStruct((m, d), f32),
        compiler_params=_params("arbitrary"),
    )(a, mm, u5, u5, x, w_pa, w_pm, w_out)


def _subln_gate(o, z, g, scale):
    y = o * lax.rsqrt(jnp.mean(o * o, axis=-1, keepdims=True) + EPS) * (g * scale)
    zf = z.astype(f32)
    return y * (zf * jax.nn.sigmoid(zf))


def _stack_maps(q, dh):
    lane = lax.broadcasted_iota(jnp.int32, q.shape, 1)
    zero = jnp.zeros_like(q)
    return jnp.concatenate([jnp.where(lane < dh, q, zero), jnp.where(lane >= dh, q, zero)], axis=0)


def _softmax_update(carry, s, vb):
    m, l, acc = carry
    m_new = jnp.maximum(m, jnp.max(s, axis=-1, keepdims=True))
    alpha = jnp.exp(m - m_new)
    p = jnp.exp(s - m_new)
    l = alpha * l + jnp.sum(p, axis=-1, keepdims=True)
    acc = alpha * acc + jnp.dot(p.astype(bf16), vb, preferred_element_type=f32)
    return m_new, l, acc


def _flash_kernel(lam_ref, q_ref, k_ref, v_ref, km_ref, vm_ref, z_ref, g_ref, o_ref, *, tq, n_meta, dh, out_scale):
    i = pl.program_id(2)
    qs = _stack_maps(q_ref[...], dh)

    def scores(kb):
        return lax.dot_general(qs, kb, NT, preferred_element_type=f32)

    s = scores(km_ref[...])
    col = lax.broadcasted_iota(jnp.int32, s.shape, 1)
    s = jnp.where(col < n_meta, s, -jnp.inf)
    m = jnp.max(s, axis=-1, keepdims=True)
    p = jnp.exp(s - m)
    l = jnp.sum(p, axis=-1, keepdims=True)
    acc = jnp.dot(p.astype(bf16), vm_ref[...], preferred_element_type=f32)

    def body(j, carry):
        off = pl.multiple_of(j * tq, tq)
        return _softmax_update(carry, scores(k_ref[pl.ds(off, tq), :]), v_ref[pl.ds(off, tq), :])

    carry = lax.fori_loop(0, i, body, (m, l, acc))
    off = pl.multiple_of(i * tq, tq)
    s = scores(k_ref[pl.ds(off, tq), :])
    row = lax.broadcasted_iota(jnp.int32, s.shape, 0)
    row = jnp.where(row >= tq, row - tq, row)
    col = lax.broadcasted_iota(jnp.int32, s.shape, 1)
    s = jnp.where(col <= row, s, -jnp.inf)
    m, l, acc = _softmax_update(carry, s, v_ref[pl.ds(off, tq), :])
    o = acc * (1.0 / l)
    out = o[:tq] - lam_ref[0] * o[tq:]
    o_ref[...] = _subln_gate(out, z_ref[...], g_ref[...], out_scale).astype(o_ref.dtype)


def _flash_attention(lam, q, kv, km, vm, z, g, *, bsz, heads, tq, n_meta, out_scale):
    mr, w = q.shape
    tr = mr // bsz
    nq = tr // tq
    hd = w // heads
    kern = functools.partial(_flash_kernel, tq=tq, n_meta=n_meta, dh=hd // 2, out_scale=out_scale)
    qspec = pl.BlockSpec((tq, hd), lambda b, h, i: (b * nq + i, h))
    return pl.pallas_call(
        kern,
        grid=(bsz, heads, nq),
        in_specs=[pl.BlockSpec(memory_space=pltpu.SMEM),
                  qspec,
                  pl.BlockSpec((tr, hd), lambda b, h, i: (b, h)),
                  pl.BlockSpec((tr, hd), lambda b, h, i: (b, heads + h)),
                  pl.BlockSpec((LANES, hd), lambda b, h, i: (0, h)),
                  pl.BlockSpec((LANES, hd), lambda b, h, i: (0, h)),
                  qspec,
                  pl.BlockSpec((1, hd), lambda b, h, i: (0, 0))],
        out_specs=qspec,
        out_shape=jax.ShapeDtypeStruct((mr, w), bf16),
        compiler_params=_params("arbitrary", "arbitrary", "arbitrary"),
    )(lam, q, kv, kv, km, vm, z, g)


def _meta_attn_kernel(lam_ref, q_ref, k_ref, v_ref, z_ref, g_ref, o_ref, *, dh, out_scale):
    r = q_ref.shape[0]
    qs = _stack_maps(q_ref[...].astype(bf16), dh)
    s = lax.dot_general(qs, k_ref[...].astype(bf16), NT, preferred_element_type=f32)
    row = lax.broadcasted_iota(jnp.int32, s.shape, 0)
    row = jnp.where(row >= r, row - r, row)
    col = lax.broadcasted_iota(jnp.int32, s.shape, 1)
    s = jnp.where(col <= row, s, -jnp.inf)
    p = jnp.exp(s - jnp.max(s, axis=-1, keepdims=True))
    l = jnp.sum(p, axis=-1, keepdims=True)
    o = jnp.dot(p.astype(bf16), v_ref[...].astype(bf16), preferred_element_type=f32) * (1.0 / l)
    out = o[:r] - lam_ref[0] * o[r:]
    o_ref[...] = _subln_gate(out, z_ref[...], g_ref[...], out_scale).astype(o_ref.dtype)


def _meta_attention(lam, q, kv, z, g, *, heads, n_meta, row0, out_scale):
    hd = q.shape[1] // heads
    rb = row0 // n_meta
    kern = functools.partial(_meta_attn_kernel, dh=hd // 2, out_scale=out_scale)
    blk = pl.BlockSpec((n_meta, hd), lambda h: (rb, h))
    return pl.pallas_call(
        kern,
        grid=(heads,),
        in_specs=[pl.BlockSpec(memory_space=pltpu.SMEM), blk, blk,
                  pl.BlockSpec((n_meta, hd), lambda h: (rb, heads + h)), blk,
                  pl.BlockSpec((1, hd), lambda h: (0, 0))],
        out_specs=pl.BlockSpec((n_meta, hd), lambda h: (0, h)),
        out_shape=jax.ShapeDtypeStruct((n_meta, q.shape[1]), f32),
        compiler_params=_params("arbitrary"),
    )(lam, q, kv, kv, z, g)


def _paged_kernel(pt_ref, lam_ref, qs_ref, *refs, n_steps, pages_per_step, heads, t_new, out_scale):
    del pt_ref
    pp = pages_per_step
    k_refs, v_refs = refs[:pp], refs[pp:2 * pp]
    kn_ref, vn_ref, z_ref, g_ref, o_ref, m_sc, l_sc, acc_sc = refs[2 * pp:]
    step = pl.program_id(1)
    rows = 2 * t_new
    hd = qs_ref.shape[-1]

    @pl.when(step == 0)
    def _():
        m_sc[...] = jnp.full(m_sc.shape, -jnp.inf, f32)
        l_sc[...] = jnp.zeros(l_sc.shape, f32)
        acc_sc[...] = jnp.zeros(acc_sc.shape, f32)

    def process(k_ref, v_ref, causal):
        ntok = k_ref.shape[0] // heads
        s = jnp.concatenate(
            [lax.dot_general(qs_ref[h], k_ref[pl.ds(h, ntok, stride=heads), :].astype(bf16), NT,
                             preferred_element_type=f32) for h in range(heads)], axis=0)
        if causal:
            t = lax.broadcasted_iota(jnp.int32, s.shape, 0) % t_new
            col = lax.broadcasted_iota(jnp.int32, s.shape, 1)
            s = jnp.where(col <= t, s, -jnp.inf)
        m_prev = m_sc[...]
        m_new = jnp.maximum(m_prev, jnp.max(s, axis=-1, keepdims=True))
        alpha = jnp.exp(m_prev - m_new)
        p = jnp.exp(s - m_new)
        l_sc[...] = alpha * l_sc[...] + jnp.sum(p, axis=-1, keepdims=True)
        m_sc[...] = m_new
        pb = p.astype(bf16)
        pv = jnp.concatenate(
            [jnp.dot(pb[h * rows:(h + 1) * rows], v_ref[pl.ds(h, ntok, stride=heads), :].astype(bf16),
                     preferred_element_type=f32) for h in range(heads)], axis=0)
        acc_sc[...] = alpha * acc_sc[...] + pv

    @pl.when(step < n_steps)
    def _():
        for j in range(pp):
            process(k_refs[j], v_refs[j], False)

    @pl.when(step == n_steps)
    def _():
        process(kn_ref, vn_ref, True)
        o = acc_sc[...] * (1.0 / l_sc[...])
        lam = lam_ref[0]
        for h in range(heads):
            oh = o[h * rows:h * rows + t_new] - lam * o[h * rows + t_new:(h + 1) * rows]
            cols = slice(h * hd, (h + 1) * hd)
            o_ref[:, cols] = _subln_gate(oh, z_ref[:, cols], g_ref[...], out_scale).astype(o_ref.dtype)


def _paged_attention(page_table, lam, qs, cache_k, cache_v, kn, vn, z, g, *, layer, heads, t_new, pages_per_step,
                     out_scale):
    dbsz, n_pages = page_table.shape
    pp = pages_per_step
    assert n_pages % pp == 0
    n_steps = n_pages // pp
    prow, hd = cache_k.shape[2], cache_k.shape[3]
    w = heads * hd
    rows = 2 * t_new

    def page_spec(j):
        def imap(b, s, pt):
            return (layer, pt[b, jnp.minimum(s, n_steps - 1) * pp + j], 0, 0)
        return pl.BlockSpec((None, None, prow, hd), imap)

    new_spec = pl.BlockSpec((None, kn.shape[1], hd), lambda b, s, pt: (b, 0, 0))
    kern = functools.partial(_paged_kernel, n_steps=n_steps, pages_per_step=pp, heads=heads, t_new=t_new,
                             out_scale=out_scale)
    grid_spec = pltpu.PrefetchScalarGridSpec(
        num_scalar_prefetch=1,
        grid=(dbsz, n_steps + 1),
        in_specs=([pl.BlockSpec(memory_space=pltpu.SMEM),
                   pl.BlockSpec((None, heads, rows, hd), lambda b, s, pt: (b, 0, 0, 0))]
                  + [page_spec(j) for j in range(pp)] + [page_spec(j) for j in range(pp)]
                  + [new_spec, new_spec,
                     pl.BlockSpec((t_new, w), lambda b, s, pt: (b, 0)),
                     pl.BlockSpec((1, hd), lambda b, s, pt: (0, 0))]),
        out_specs=pl.BlockSpec((t_new, w), lambda b, s, pt: (b, 0)),
        scratch_shapes=[pltpu.VMEM((heads * rows, 1), f32), pltpu.VMEM((heads * rows, 1), f32),
                        pltpu.VMEM((heads * rows, hd), f32)],
    )
    return pl.pallas_call(
        kern,
        grid_spec=grid_spec,
        out_shape=jax.ShapeDtypeStruct((dbsz * t_new, w), f32),
        compiler_params=_params("arbitrary", "arbitrary"),
    )(page_table, lam, qs, *([cache_k] * pp), *([cache_v] * pp), kn, vn, z, g)


def _mlstm_kernel(m0_ref, q_ref, k_ref, v_ref, gi_ref, gf_ref, c0_ref, n0_ref, og_ref, zg_ref, hg_ref,
                  h_ref, c_out, n_out, m_out, c_sc, n_sc, m_sc, *, heads, shared_init):
    b, h, c = pl.program_id(0), pl.program_id(1), pl.program_id(2)
    chunk = q_ref.shape[0]

    @pl.when(c == 0)
    def _():
        c_sc[...] = c0_ref[...]
        n_sc[...] = n0_ref[...]
        m_sc[...] = jnp.full(m_sc.shape, m0_ref[h if shared_init else b * heads + h], f32)

    q, k, v = q_ref[...].astype(bf16), k_ref[...].astype(bf16), v_ref[...].astype(bf16)
    li, lf = gi_ref[...], gf_ref[...]
    r = lax.broadcasted_iota(jnp.int32, (chunk, chunk), 0)
    s = lax.broadcasted_iota(jnp.int32, (chunk, chunk), 1)
    tri, eye = s <= r, s == r
    lf_col = jnp.sum(jnp.where(eye, lf, 0.0), axis=1, keepdims=True)
    li_col = jnp.sum(jnp.where(eye, li, 0.0), axis=1, keepdims=True)
    b_col = jnp.sum(jnp.where(tri, lf, 0.0), axis=1, keepdims=True)
    b_row = jnp.sum(jnp.where(r <= s, lf_col, 0.0), axis=0, keepdims=True)
    m_prev = m_sc[:, :1]
    a_col = b_col + m_prev
    d = jnp.where(tri, b_col - b_row + li, -jnp.inf)
    mt = jnp.maximum(a_col, jnp.max(d, axis=1, keepdims=True))
    w_prev = jnp.exp(a_col - mt)
    sm = lax.dot_general(q, k, NT, preferred_element_type=f32) * jnp.exp(d - mt)
    qc = lax.dot_general(q, c_sc[...].astype(bf16), NT, preferred_element_type=f32)
    num = jnp.dot(sm.astype(bf16), v, preferred_element_type=f32) + w_prev * qc
    qn = jnp.sum(q.astype(f32) * n_sc[...], axis=1, keepdims=True)
    den = jnp.sum(sm, axis=1, keepdims=True) + w_prev * qn
    hh = num / jnp.maximum(jnp.abs(den), jnp.exp(-mt))

    b_end = jnp.sum(lf, axis=1, keepdims=True)
    a_end = b_end + m_prev
    m_new = jnp.maximum(a_end, jnp.max(b_end - b_row + li, axis=1, keepdims=True))
    w_old = jnp.exp(a_end - m_new)
    w_tok = jnp.exp(b_end - b_col + li_col - m_new)
    vw = (v.astype(f32) * w_tok).astype(bf16)
    c_sc[...] = w_old * c_sc[...] + lax.dot_general(vw, k, TN, preferred_element_type=f32)
    n_sc[...] = w_old * n_sc[...] + jnp.sum(k.astype(f32) * w_tok, axis=0, keepdims=True)
    m_sc[...] = jnp.broadcast_to(m_new, m_sc.shape)

    hn = hh * lax.rsqrt(jnp.mean(hh * hh, axis=1, keepdims=True) + EPS) * hg_ref[...]
    zg = zg_ref[...].astype(f32)
    h_ref[...] = (hn * jax.nn.sigmoid(og_ref[...].astype(f32)) * (zg * jax.nn.sigmoid(zg))).astype(h_ref.dtype)

    @pl.when(c == pl.num_programs(2) - 1)
    def _():
        c_out[...] = c_sc[...]
        n_out[...] = n_sc[...]
        m_out[...] = m_sc[...]


def _mlstm(qk, u5, gi, gf, c0, n0, m0, hg, *, bsz, heads, chunk, row0, out_dtype):
    nc = gi.shape[2]
    dk = qk.shape[1] // (2 * heads)
    dv = c0.shape[2]
    rb = row0 // chunk
    shared = c0.shape[0] == 1
    rows = bsz * nc * chunk

    def rmap(cb):
        return lambda b, h, c: (rb + b * nc + c, cb(h))

    def smap(b, h, c):
        return (0 if shared else b, h, 0, 0)

    gate = pl.BlockSpec((None, None, None, 1, chunk), lambda b, h, c: (b, h, c, 0, 0))
    kern = functools.partial(_mlstm_kernel, heads=heads, shared_init=shared)
    return pl.pallas_call(
        kern,
        grid=(bsz, heads, nc),
        in_specs=[pl.BlockSpec(memory_space=pltpu.SMEM),
                  pl.BlockSpec((chunk, dk), rmap(lambda h: h)),
                  pl.BlockSpec((chunk, dk), rmap(lambda h: heads + h)),
                  pl.BlockSpec((chunk, dv), rmap(lambda h: h)),
                  gate, gate,
                  pl.BlockSpec((None, None, dv, dk), smap),
                  pl.BlockSpec((None, None, 1, dk), smap),
                  pl.BlockSpec((chunk, dv), rmap(lambda h: heads + h)),
                  pl.BlockSpec((chunk, dv), rmap(lambda h: 2 * heads + h)),
                  pl.BlockSpec((1, dv), lambda b, h, c: (0, h))],
        out_specs=[pl.BlockSpec((chunk, dv), lambda b, h, c: (b * nc + c, h)),
                   pl.BlockSpec((None, None, dv, dk), lambda b, h, c: (b, h, 0, 0)),
                   pl.BlockSpec((None, None, 1, dk), lambda b, h, c: (b, h, 0, 0)),
                   pl.BlockSpec((None, None, 1, LANES), lambda b, h, c: (b, h, 0, 0))],
        out_shape=[jax.ShapeDtypeStruct((rows, heads * dv), out_dtype),
                   jax.ShapeDtypeStruct((bsz, heads, dv, dk), f32),
                   jax.ShapeDtypeStruct((bsz, heads, 1, dk), f32),
                   jax.ShapeDtypeStruct((bsz, heads, 1, LANES), f32)],
        scratch_shapes=[pltpu.VMEM((dv, dk), f32), pltpu.VMEM((1, dk), f32), pltpu.VMEM((1, LANES), f32)],
        compiler_params=_params("arbitrary", "arbitrary", "arbitrary"),
    )(m0, qk, qk, u5, gi, gf, c0, n0, u5, u5, hg)


def _rms(x, g):
    return x * lax.rsqrt(jnp.mean(x * x, axis=-1, keepdims=True) + EPS) * g


def _conv_silu(u, w, b, prev, q_width, q_scale):
    t = u.shape[1]
    full = jnp.concatenate([prev, u], axis=1)
    out = b + full[:, 0:t] * w[0]
    for j in range(1, w.shape[0]):
        out = out + full[:, j:j + t] * w[j]
    out = jax.nn.silu(out)
    scale = jnp.where(jnp.arange(out.shape[-1]) < q_width, q_scale, 1.0).astype(f32)
    return out * scale


def _gates(g, b_i, b_f, heads, bsz, chunk):
    gi = g[:, :heads] + b_i
    gf = jax.nn.log_sigmoid(g[:, heads:2 * heads] + b_f)

    def lay(a):
        a = a.reshape(bsz, -1, heads).transpose(0, 2, 1)
        return a.reshape(bsz, heads, -1, 1, chunk)
    return lay(gi), lay(gf)


def kernel(x_prompt, x_sample, cache_k, cache_v, page_table, state_conv, state_C, state_n, state_m, meta, norm_g,
           w_in, b_i, b_f, conv_w, conv_b, lam_q1, lam_k1, lam_q2, lam_k2, subln_g, head_g, w_pa, w_pm, w_out,
           norm_f):
    bsz, seq, d = x_prompt.shape
    dbsz, t_new, _ = x_sample.shape
    depth, n_phys, page, h_a, hd_a = cache_k.shape
    dh_a = hd_a // 2
    w_a = h_a * hd_a
    h_m, dv_m, dk_m = state_C.shape[2], state_C.shape[3], state_C.shape[4]
    w_qk, w_m = h_m * dk_m, h_m * dv_m
    n_meta = meta.shape[0]
    n_s = dbsz * t_new
    mr = bsz * seq

    c_gate = 4 * w_a + 2 * w_qk + 3 * w_m
    qcol = (jnp.arange(c_gate + 2 * d) < w_a)
    wb = jnp.concatenate([w_in[:, :, :c_gate], w_in[:, :, c_gate + 2 * h_m:]], axis=-1)
    wb = (wb * jnp.where(qcol, dh_a ** -0.5, 1.0).astype(f32)).astype(bf16)
    wg = jnp.pad(w_in[:, :, c_gate:c_gate + 2 * h_m], ((0, 0), (0, 0), (0, LANES - 2 * h_m))).astype(bf16)
    wpa, wpm, wo = w_pa.astype(bf16), w_pm.astype(bf16), w_out.astype(bf16)
    o_q, o_kv, o_z, o_mqk, o_u5 = 0, w_a, 3 * w_a, 4 * w_a, 4 * w_a + 2 * w_qk
    n_u5 = 3 * w_m + 2 * d
    ga_blk, gm_blk = (3 * w_m) // d, (3 * w_m) // d + 1

    ck = cache_k.reshape(depth, n_phys, page * h_a, hd_a)
    cv = cache_v.reshape(depth, n_phys, page * h_a, hd_a)

    x_m = x_prompt.reshape(mr, d)
    x_s = jnp.concatenate([x_sample.reshape(n_s, d), meta.astype(f32)], axis=0)
    chunk = 256 if seq % 256 == 0 else seq
    tq = 256 if seq % 256 == 0 else seq

    outs = {k: [] for k in ("kp", "vp", "ks", "vs", "cp", "cs", "Cp", "np", "mp", "Cs", "ns", "ms")}
    for l in range(depth):
        lam_init = 0.8 - 0.6 * math.exp(-0.3 * l)
        lam = (jnp.exp(jnp.sum(lam_q1[l] * lam_k1[l])) - jnp.exp(jnp.sum(lam_q2[l] * lam_k2[l]))
               + lam_init).reshape(1).astype(f32)
        out_scale = 1.0 - lam_init
        sg = subln_g[l].reshape(1, hd_a)
        hg = head_g[l].reshape(1, w_m)

        xn_s = _rms(x_s, norm_g[l]).astype(bf16)
        q_s, = _matmul(xn_s, wb, l, o_q, w_a, [f32], 512, 1024)
        kv_s, = _matmul(xn_s, wb, l, o_kv, 2 * w_a, [f32], 512, 1024)
        z_s, = _matmul(xn_s, wb, l, o_z, w_a, [f32], 512, 1024)
        mqk_s, = _matmul(xn_s, wb, l, o_mqk, 2 * w_qk, [f32], 512, 1024)
        u5_s, = _matmul(xn_s, wb, l, o_u5, n_u5, [f32], 512, 1024)
        g_s, = _matmul(xn_s, wg, l, 0, LANES, [f32], 512, LANES)

        mqk_samp = mqk_s[:n_s].reshape(dbsz, t_new, 2 * w_qk)
        mqk_meta = mqk_s[n_s:].reshape(1, n_meta, 2 * w_qk)
        qk_samp = _conv_silu(mqk_samp, conv_w[l], conv_b[l], state_conv[l], w_qk, dk_m ** -0.5)
        qk_meta = _conv_silu(mqk_meta, conv_w[l], conv_b[l], jnp.zeros((1, conv_w.shape[1] - 1, 2 * w_qk), f32),
                             w_qk, dk_m ** -0.5)
        qk_s = jnp.concatenate([qk_samp.reshape(n_s, -1), qk_meta.reshape(n_meta, -1)], axis=0)
        outs["cs"].append(jnp.concatenate([state_conv[l], mqk_samp], axis=1)[:, -(conv_w.shape[1] - 1):])

        gi_samp, gf_samp = _gates(g_s[:n_s], b_i[l], b_f[l], h_m, dbsz, t_new)
        gi_meta, gf_meta = _gates(g_s[n_s:], b_i[l], b_f[l], h_m, 1, n_meta)
        hm_samp, c_samp, n_samp, m_samp = _mlstm(
            qk_s, u5_s, gi_samp, gf_samp, state_C[l], state_n[l].reshape(dbsz, h_m, 1, dk_m),
            state_m[l].reshape(-1), hg, bsz=dbsz, heads=h_m, chunk=t_new, row0=0, out_dtype=f32)
        hm_meta, c_meta, n_meta_s, m_meta = _mlstm(
            qk_s, u5_s, gi_meta, gf_meta, jnp.zeros((1, h_m, dv_m, dk_m), f32), jnp.zeros((1, h_m, 1, dk_m), f32),
            jnp.zeros((h_m,), f32), hg, bsz=1, heads=h_m, chunk=n_meta, row0=n_s, out_dtype=f32)
        outs["Cs"].append(c_samp)
        outs["ns"].append(n_samp.reshape(dbsz, h_m, dk_m))
        outs["ms"].append(m_samp[:, :, 0, 0])

        qs = q_s[:n_s].reshape(dbsz, t_new, h_a, hd_a).transpose(0, 2, 1, 3)
        lane = jnp.arange(hd_a)
        qs = jnp.concatenate([jnp.where(lane < dh_a, qs, 0.0), jnp.where(lane >= dh_a, qs, 0.0)], axis=2).astype(bf16)

        def new_page(a):
            a = a.reshape(dbsz, t_new, h_a, hd_a)
            a = jnp.pad(a, ((0, 0), (0, page - t_new), (0, 0), (0, 0)))
            return a.reshape(dbsz, page * h_a, hd_a)
        att_samp = _paged_attention(page_table, lam, qs, ck, cv, new_page(kv_s[:n_s, :w_a]),
                                    new_page(kv_s[:n_s, w_a:]), z_s, sg, layer=l, heads=h_a, t_new=t_new,
                                    pages_per_step=2, out_scale=out_scale)
        att_meta = _meta_attention(lam, q_s, kv_s, z_s, sg, heads=h_a, n_meta=n_meta, row0=n_s, out_scale=out_scale)
        att_s = jnp.concatenate([att_samp, att_meta], axis=0)
        hm_s = jnp.concatenate([hm_samp, hm_meta], axis=0)
        x_s_new = _out_proj(att_s, hm_s, u5_s, ga_blk, gm_blk, x_s, wpa, wpm, wo, l, 256)
        outs["ks"].append(kv_s[:n_s, :w_a].reshape(dbsz, t_new, h_a, hd_a))
        outs["vs"].append(kv_s[:n_s, w_a:].reshape(dbsz, t_new, h_a, hd_a))

        xn_m = _rms(x_m, norm_g[l]).astype(bf16)
        q_m, = _matmul(xn_m, wb, l, o_q, w_a, [bf16], 512, 1024)
        kv_m, kv16_m = _matmul(xn_m, wb, l, o_kv, 2 * w_a, [f32, bf16], 512, 1024)
        z_m, = _matmul(xn_m, wb, l, o_z, w_a, [bf16], 512, 1024)
        mqk_m, = _matmul(xn_m, wb, l, o_mqk, 2 * w_qk, [f32], 512, 1024)
        u5_m, = _matmul(xn_m, wb, l, o_u5, n_u5, [bf16], 512, 1024)
        g_m, = _matmul(xn_m, wg, l, 0, LANES, [f32], 512, LANES)

        mqk_m3 = mqk_m.reshape(bsz, seq, 2 * w_qk)
        prev = jnp.broadcast_to(mqk_meta[:, -(conv_w.shape[1] - 1):], (bsz, conv_w.shape[1] - 1, 2 * w_qk))
        qk_m = _conv_silu(mqk_m3, conv_w[l], conv_b[l], prev, w_qk, dk_m ** -0.5).reshape(mr, -1).astype(bf16)
        outs["cp"].append(mqk_m3[:, -(conv_w.shape[1] - 1):])
        gi_m, gf_m = _gates(g_m, b_i[l], b_f[l], h_m, bsz, chunk)
        hm_m, c_p, n_p, m_p = _mlstm(qk_m, u5_m, gi_m, gf_m, c_meta, n_meta_s, m_meta[:, :, 0, 0].reshape(-1), hg,
                                     bsz=bsz, heads=h_m, chunk=chunk, row0=0, out_dtype=bf16)
        outs["Cp"].append(c_p)
        outs["np"].append(n_p.reshape(bsz, h_m, dk_m))
        outs["mp"].append(m_p[:, :, 0, 0])

        pad_meta = ((0, LANES - n_meta), (0, 0))
        km = jnp.pad(kv_s[n_s:, :w_a], pad_meta).astype(bf16)
        vm = jnp.pad(kv_s[n_s:, w_a:], pad_meta).astype(bf16)
        att_m = _flash_attention(lam, q_m, kv16_m, km, vm, z_m, sg, bsz=bsz, heads=h_a, tq=tq, n_meta=n_meta,
                                 out_scale=out_scale)
        x_m_new = _out_proj(att_m, hm_m, u5_m, ga_blk, gm_blk, x_m, wpa, wpm, wo, l, 256)

        def with_meta(real, meta_rows):
            real = real.reshape(bsz, seq, h_a, hd_a)
            meta_rows = jnp.broadcast_to(meta_rows.reshape(1, n_meta, h_a, hd_a), (bsz, n_meta, h_a, hd_a))
            return jnp.concatenate([meta_rows, real], axis=1)
        outs["kp"].append(with_meta(kv_m[:, :w_a], kv_s[n_s:, :w_a]))
        outs["vp"].append(with_meta(kv_m[:, w_a:], kv_s[n_s:, w_a:]))
        x_m, x_s = x_m_new, x_s_new

    y_prompt = _rms(x_m, norm_f).reshape(bsz, seq, d)
    y_sample = _rms(x_s[:n_s], norm_f).reshape(dbsz, t_new, d)
    st = {k: jnp.stack(v) for k, v in outs.items()}
    return (y_prompt, y_sample, st["kp"], st["vp"], st["ks"], st["vs"], st["cp"], st["cs"],
            st["Cp"], st["np"], st["mp"], st["Cs"], st["ns"], st["ms"])
```

```python
import functools
import math

import jax
import jax.numpy as jnp
from jax import lax
from jax.experimental import pallas as pl
from jax.experimental.pallas import tpu as pltpu

f32 = jnp.float32
bf16 = jnp.bfloat16

EPS = 1e-6
LANES = 128
SUBLANES = 8
ONES_ROWS = 16
LOG2E = math.log2(math.e)
V7X_VMEM_LIMIT = 56 * 1024 * 1024
NT = (((1,), (1,)), ((), ()))
TN = (((0,), (0,)), ((), ()))


def _params(*sem):
    return pltpu.CompilerParams(dimension_semantics=sem, vmem_limit_bytes=V7X_VMEM_LIMIT)


def _mm_kernel(x_ref, w_ref, *refs, scale):
    o_refs, wb_sc = refs[:-1], refs[-1]

    @pl.when(pl.program_id(1) == 0)
    def _():
        wb_sc[...] = w_ref[...].astype(bf16)

    acc = jnp.dot(x_ref[...], wb_sc[...], preferred_element_type=f32)
    if scale != 1.0:
        acc = acc * scale
    for o_ref in o_refs:
        o_ref[...] = acc.astype(o_ref.dtype)


def _matmul(name, x, w, layer, col0, ncols, out_dtypes, tm, tn, scale=1.0):
    m, k = x.shape
    tm = min(tm, m)
    tn = min(tn, ncols)
    assert col0 % tn == 0 and ncols % tn == 0
    return pl.pallas_call(
        functools.partial(_mm_kernel, scale=scale),
        grid=(ncols // tn, pl.cdiv(m, tm)),
        in_specs=[pl.BlockSpec((tm, k), lambda j, i: (i, 0)),
                  pl.BlockSpec((None, k, tn), lambda j, i: (layer, 0, col0 // tn + j))],
        out_specs=[pl.BlockSpec((tm, tn), lambda j, i: (i, j)) for _ in out_dtypes],
        out_shape=[jax.ShapeDtypeStruct((m, ncols), dt) for dt in out_dtypes],
        scratch_shapes=[pltpu.VMEM((k, tn), bf16)],
        compiler_params=_params("arbitrary", "arbitrary"),
        name=name,
    )(x, w)


def _out_kernel(a_ref, m_ref, ga_ref, gm_ref, x_ref, wpa_ref, wpm_ref, wo_ref, gn_ref, o_ref, xn_ref):
    ya = jnp.dot(a_ref[...].astype(bf16), wpa_ref[...], preferred_element_type=f32)
    ym = jnp.dot(m_ref[...].astype(bf16), wpm_ref[...], preferred_element_type=f32)
    merged = (jax.nn.sigmoid(ga_ref[...].astype(f32)) * ya
              + jax.nn.sigmoid(gm_ref[...].astype(f32)) * ym)
    x_new = x_ref[...] + jnp.dot(merged.astype(bf16), wo_ref[...], preferred_element_type=f32)
    o_ref[...] = x_new
    xn = x_new * lax.rsqrt(jnp.mean(x_new * x_new, axis=-1, keepdims=True) + EPS) * gn_ref[...]
    xn_ref[...] = xn.astype(xn_ref.dtype)


def _out_proj(name, a, mm, g, x, w_pa, w_pm, w_out, g_next, layer, tm, xn_dtype):
    m, d = x.shape
    tm = min(tm, m)
    wspec = pl.BlockSpec((None, d, d), lambda i: (layer, 0, 0), pipeline_mode=pl.Buffered(1))
    row = pl.BlockSpec((tm, d), lambda i: (i, 0))
    return pl.pallas_call(
        _out_kernel,
        grid=(pl.cdiv(m, tm),),
        in_specs=[row, row, row, pl.BlockSpec((tm, d), lambda i: (i, 1)), row, wspec, wspec, wspec,
                  pl.BlockSpec((1, d), lambda i: (0, 0))],
        out_specs=[row, row],
        out_shape=[jax.ShapeDtypeStruct((m, d), f32), jax.ShapeDtypeStruct((m, d), xn_dtype)],
        compiler_params=_params("arbitrary"),
        name=name,
    )(a, mm, g, g, x, w_pa, w_pm, w_out, g_next)


def _subln_gate(o, z, g, scale):
    y = o * lax.rsqrt(jnp.mean(o * o, axis=-1, keepdims=True) + EPS) * (g * scale)
    zf = z.astype(f32)
    return y * (zf * jax.nn.sigmoid(zf))


def _stack_maps(q, dh):
    lane = lax.broadcasted_iota(jnp.int32, q.shape, 1)
    zero = jnp.zeros_like(q)
    return jnp.concatenate([jnp.where(lane < dh, q, zero), jnp.where(lane >= dh, q, zero)], axis=0)


def _flash_kernel(lam_ref, q_ref, k_ref, v_ref, km_ref, vmt_ref, z_ref, g_ref, o_ref, qs_sc, vt_sc, m_sc, acc_sc,
                  *, tq, hd, hb, out_scale):
    i = pl.program_id(2)
    tr = k_ref.shape[0]
    dh = hd // 2
    ones = jnp.ones((ONES_ROWS, tq), bf16)

    @pl.when(i == 0)
    def _():
        for hh in range(hb):
            for c in range(tr // tq):
                vt = v_ref[c * tq:(c + 1) * tq, hh * hd:(hh + 1) * hd].astype(f32).T.astype(bf16)
                vt_sc[hh, :, c * tq:(c + 1) * tq] = jnp.concatenate([vt, ones], axis=0)

    def scores_t(hh, kb):
        return lax.dot_general(kb, qs_sc[hh], NT, preferred_element_type=f32)

    n_meta = km_ref.shape[0]
    for hh in range(hb):
        cols = slice(hh * hd, (hh + 1) * hd)
        qs_sc[hh] = _stack_maps(q_ref[:, cols], dh)
        s = scores_t(hh, km_ref[:, cols])
        m = jnp.max(s, axis=0, keepdims=True)
        m_sc[hh] = m
        vmt = jnp.concatenate([vmt_ref[cols, :], jnp.ones((ONES_ROWS, n_meta), bf16)], axis=0)
        acc_sc[hh] = jnp.dot(vmt, jnp.exp2(s - m).astype(bf16), preferred_element_type=f32)

    def update(off, visible=None):
        hs = range(hb)
        ss = [scores_t(hh, k_ref[pl.ds(off, tq), hh * hd:(hh + 1) * hd]) for hh in hs]
        if visible is not None:
            ss = [jnp.where(visible, s, -jnp.inf) for s in ss]
        m_prev = [m_sc[hh] for hh in hs]
        m_new = [jnp.maximum(m_prev[hh], jnp.max(ss[hh], axis=0, keepdims=True)) for hh in hs]
        ps = [jnp.exp2(ss[hh] - m_new[hh]).astype(bf16) for hh in hs]
        pvs = [jnp.dot(vt_sc[hh, :, pl.ds(off, tq)], ps[hh], preferred_element_type=f32) for hh in hs]
        for hh in hs:
            acc_sc[hh] = jnp.exp2(m_prev[hh] - m_new[hh]) * acc_sc[hh] + pvs[hh]
            m_sc[hh] = m_new[hh]

    def body(j, carry):
        update(pl.multiple_of(j * tq, tq))
        return carry

    lax.fori_loop(0, i, body, 0)
    key = lax.broadcasted_iota(jnp.int32, (tq, 2 * tq), 0)
    qry = lax.broadcasted_iota(jnp.int32, (tq, 2 * tq), 1)
    update(pl.multiple_of(i * tq, tq), key <= jnp.where(qry >= tq, qry - tq, qry))
    for hh in range(hb):
        cols = slice(hh * hd, (hh + 1) * hd)
        acc = acc_sc[hh]
        o = acc[:hd] * (1.0 / acc[hd:hd + 1])
        out = (o[:, :tq] - lam_ref[0] * o[:, tq:]).T
        o_ref[:, cols] = _subln_gate(out, z_ref[:, cols], g_ref[...], out_scale).astype(o_ref.dtype)


def _flash_attention(name, lam, q, kv, km, vmt, z, g, *, bsz, heads, tq, hb, out_scale):
    mr, w = q.shape
    tr = mr // bsz
    nq = tr // tq
    hd = w // heads
    n_meta = km.shape[0]
    ng = heads // hb
    kern = functools.partial(_flash_kernel, tq=tq, hd=hd, hb=hb, out_scale=out_scale)
    qspec = pl.BlockSpec((tq, hb * hd), lambda b, h, i: (b * nq + i, h))
    return pl.pallas_call(
        kern,
        grid=(bsz, ng, nq),
        in_specs=[pl.BlockSpec(memory_space=pltpu.SMEM),
                  qspec,
                  pl.BlockSpec((tr, hb * hd), lambda b, h, i: (b, h)),
                  pl.BlockSpec((tr, hb * hd), lambda b, h, i: (b, ng + h)),
                  pl.BlockSpec((n_meta, hb * hd), lambda b, h, i: (0, h)),
                  pl.BlockSpec((hb * hd, n_meta), lambda b, h, i: (h, 0)),
                  qspec,
                  pl.BlockSpec((1, hd), lambda b, h, i: (0, 0))],
        out_specs=qspec,
        out_shape=jax.ShapeDtypeStruct((mr, w), bf16),
        scratch_shapes=[pltpu.VMEM((hb, 2 * tq, hd), bf16), pltpu.VMEM((hb, hd + ONES_ROWS, tr), bf16),
                        pltpu.VMEM((hb, 1, 2 * tq), f32), pltpu.VMEM((hb, hd + ONES_ROWS, 2 * tq), f32)],
        compiler_params=_params("arbitrary", "arbitrary", "arbitrary"),
        name=name,
    )(lam, q, kv, kv, km, vmt, z, g)


def _meta_attn_kernel(lam_ref, q_ref, k_ref, v_ref, z_ref, g_ref, o_ref, *, dh, out_scale):
    r = q_ref.shape[0]
    qs = _stack_maps(q_ref[...].astype(bf16), dh)
    s = lax.dot_general(qs, k_ref[...].astype(bf16), NT, preferred_element_type=f32)
    row = lax.broadcasted_iota(jnp.int32, s.shape, 0)
    row = jnp.where(row >= r, row - r, row)
    col = lax.broadcasted_iota(jnp.int32, s.shape, 1)
    s = jnp.where(col <= row, s, -jnp.inf)
    p = jnp.exp(s - jnp.max(s, axis=-1, keepdims=True))
    l = jnp.sum(p, axis=-1, keepdims=True)
    o = jnp.dot(p.astype(bf16), v_ref[...].astype(bf16), preferred_element_type=f32) * (1.0 / l)
    out = o[:r] - lam_ref[0] * o[r:]
    o_ref[...] = _subln_gate(out, z_ref[...], g_ref[...], out_scale).astype(o_ref.dtype)


def _meta_attention(name, lam, q, kv, z, g, *, heads, n_meta, row0, out_scale):
    hd = q.shape[1] // heads
    rb = row0 // n_meta
    kern = functools.partial(_meta_attn_kernel, dh=hd // 2, out_scale=out_scale)
    blk = pl.BlockSpec((n_meta, hd), lambda h: (rb, h))
    return pl.pallas_call(
        kern,
        grid=(heads,),
        in_specs=[pl.BlockSpec(memory_space=pltpu.SMEM), blk, blk,
                  pl.BlockSpec((n_meta, hd), lambda h: (rb, heads + h)), blk,
                  pl.BlockSpec((1, hd), lambda h: (0, 0))],
        out_specs=pl.BlockSpec((n_meta, hd), lambda h: (0, h)),
        out_shape=jax.ShapeDtypeStruct((n_meta, q.shape[1]), f32),
        compiler_params=_params("arbitrary"),
        name=name,
    )(lam, q, kv, kv, z, g)


def _rows_of_head(ref, hh):
    ntok, _, hd = ref.shape
    return ref.reshape(ntok * SUBLANES, hd)[pl.ds(hh, ntok, stride=SUBLANES), :]


def _paged_kernel(pt_ref, lam_ref, qs_ref, *refs, n_steps, pages_per_step, groups, t_new, out_scale):
    del pt_ref
    npg = pages_per_step * groups
    k_refs, v_refs = refs[:npg], refs[npg:2 * npg]
    kn_refs, vn_refs = refs[2 * npg:2 * npg + groups], refs[2 * npg + groups:2 * npg + 2 * groups]
    z_ref, g_ref, o_ref, m_sc, l_sc, acc_sc = refs[2 * npg + 2 * groups:]
    step = pl.program_id(1)
    rows = 2 * t_new
    heads = groups * SUBLANES
    hd = qs_ref.shape[-1]

    @pl.when(step == 0)
    def _():
        m_sc[...] = jnp.full(m_sc.shape, -jnp.inf, f32)
        l_sc[...] = jnp.zeros(l_sc.shape, f32)
        acc_sc[...] = jnp.zeros(acc_sc.shape, f32)

    def head_rows(ref_list, h):
        return _rows_of_head(ref_list[h // SUBLANES], h % SUBLANES).astype(bf16)

    def process(kpages, vpages, causal):
        ntok = kpages[0][0].shape[0]
        s = jnp.concatenate(
            [jnp.concatenate([lax.dot_general(qs_ref[h], head_rows(kg, h), NT, preferred_element_type=f32)
                              for kg in kpages], axis=1) for h in range(heads)], axis=0)
        if causal:
            t = lax.broadcasted_iota(jnp.int32, s.shape, 0) % t_new
            col = lax.broadcasted_iota(jnp.int32, s.shape, 1)
            s = jnp.where(col <= t, s, -jnp.inf)
        m_prev = m_sc[...]
        m_new = jnp.maximum(m_prev, jnp.max(s, axis=-1, keepdims=True))
        alpha = jnp.exp(m_prev - m_new)
        p = jnp.exp(s - m_new)
        l_sc[...] = alpha * l_sc[...] + jnp.sum(p, axis=-1, keepdims=True)
        m_sc[...] = m_new
        pb = p.astype(bf16)
        pv = jnp.concatenate(
            [sum(jnp.dot(pb[h * rows:(h + 1) * rows, j * ntok:(j + 1) * ntok], head_rows(vg, h),
                         preferred_element_type=f32) for j, vg in enumerate(vpages))
             for h in range(heads)], axis=0)
        acc_sc[...] = alpha * acc_sc[...] + pv

    @pl.when(step < n_steps)
    def _():
        process([k_refs[j * groups:(j + 1) * groups] for j in range(pages_per_step)],
                [v_refs[j * groups:(j + 1) * groups] for j in range(pages_per_step)], False)

    @pl.when(step == n_steps)
    def _():
        process([kn_refs], [vn_refs], True)
        o = acc_sc[...] * (1.0 / l_sc[...])
        lam = lam_ref[0]
        for h in range(heads):
            oh = o[h * rows:h * rows + t_new] - lam * o[h * rows + t_new:(h + 1) * rows]
            cols = slice(h * hd, (h + 1) * hd)
            o_ref[:, cols] = _subln_gate(oh, z_ref[:, cols], g_ref[...], out_scale).astype(o_ref.dtype)


def _paged_attention(name, page_table, lam, qs, cache_k, cache_v, kn, vn, z, g, *, layer, t_new, pages_per_step,
                     out_scale):
    dbsz, n_pages = page_table.shape
    pp = pages_per_step
    assert n_pages % pp == 0
    n_steps = n_pages // pp
    page, groups, hd = cache_k.shape[2], cache_k.shape[3], cache_k.shape[5]
    heads = groups * SUBLANES
    w = heads * hd
    rows = 2 * t_new

    def page_spec(j, gidx):
        def imap(b, s, pt):
            return (layer, pt[b, jnp.minimum(s, n_steps - 1) * pp + j], 0, gidx, 0, 0)
        return pl.BlockSpec((None, None, page, None, SUBLANES, hd), imap)

    def new_spec(gidx):
        return pl.BlockSpec((None, kn.shape[1], None, SUBLANES, hd), lambda b, s, pt: (b, 0, gidx, 0, 0))

    page_specs = [page_spec(j, gidx) for j in range(pp) for gidx in range(groups)]
    new_specs = [new_spec(gidx) for gidx in range(groups)]
    kern = functools.partial(_paged_kernel, n_steps=n_steps, pages_per_step=pp, groups=groups, t_new=t_new,
                             out_scale=out_scale)
    grid_spec = pltpu.PrefetchScalarGridSpec(
        num_scalar_prefetch=1,
        grid=(dbsz, n_steps + 1),
        in_specs=([pl.BlockSpec(memory_space=pltpu.SMEM),
                   pl.BlockSpec((None, heads, rows, hd), lambda b, s, pt: (b, 0, 0, 0))]
                  + page_specs + page_specs + new_specs + new_specs
                  + [pl.BlockSpec((t_new, w), lambda b, s, pt: (b, 0)),
                     pl.BlockSpec((1, hd), lambda b, s, pt: (0, 0))]),
        out_specs=pl.BlockSpec((t_new, w), lambda b, s, pt: (b, 0)),
        scratch_shapes=[pltpu.VMEM((heads * rows, 1), f32), pltpu.VMEM((heads * rows, 1), f32),
                        pltpu.VMEM((heads * rows, hd), f32)],
    )
    return pl.pallas_call(
        kern,
        grid_spec=grid_spec,
        out_shape=jax.ShapeDtypeStruct((dbsz * t_new, w), f32),
        compiler_params=_params("arbitrary", "arbitrary"),
        name=name,
    )(page_table, lam, qs, *([cache_k] * (pp * groups)), *([cache_v] * (pp * groups)),
      *([kn] * groups), *([vn] * groups), z, g)


def _mlstm_kernel(m0_ref, uq_ref, uk_ref, pq_ref, pk_ref, wq_ref, wk_ref, bq_ref, bk_ref, v_ref, gi_ref, gf_ref,
                  c0_ref, n0_ref, og_ref, zg_ref, hg_ref, h_ref, c_out, n_out, m_out,
                  padq_sc, padk_sc, c_sc, n_sc, m_sc, *, heads, shared_init, q_scale):
    b, h, c = pl.program_id(0), pl.program_id(1), pl.program_id(2)
    chunk = uq_ref.shape[0]
    taps = wq_ref.shape[0]
    base = SUBLANES - (taps - 1)

    @pl.when(c == 0)
    def _():
        c_sc[...] = c0_ref[...]
        n_sc[...] = n0_ref[...]
        m_sc[...] = jnp.full(m_sc.shape, m0_ref[h if shared_init else b * heads + h], f32)
        padq_sc[base:SUBLANES, :] = pq_ref[...]
        padk_sc[base:SUBLANES, :] = pk_ref[...]

    def conv_silu(u_ref, w_ref, b_ref, pad_sc):
        pad_sc[SUBLANES:SUBLANES + chunk, :] = u_ref[...]
        out = b_ref[...] + pad_sc[base:base + chunk, :] * w_ref[0:1, :]
        for j in range(1, taps):
            out = out + pad_sc[base + j:base + j + chunk, :] * w_ref[j:j + 1, :]
        pad_sc[base:SUBLANES, :] = pad_sc[base + chunk:SUBLANES + chunk, :]
        return out * jax.nn.sigmoid(out)

    q = (conv_silu(uq_ref, wq_ref, bq_ref, padq_sc) * q_scale).astype(bf16)
    k = conv_silu(uk_ref, wk_ref, bk_ref, padk_sc).astype(bf16)
    v = v_ref[...].astype(bf16)
    li, lf = gi_ref[...], gf_ref[...]
    r = lax.broadcasted_iota(jnp.int32, (chunk, chunk), 0)
    s = lax.broadcasted_iota(jnp.int32, (chunk, chunk), 1)
    tri, eye = s <= r, s == r
    lf_col = jnp.sum(jnp.where(eye, lf, 0.0), axis=1, keepdims=True)
    li_col = jnp.sum(jnp.where(eye, li, 0.0), axis=1, keepdims=True)
    b_col = jnp.sum(jnp.where(tri, lf, 0.0), axis=1, keepdims=True)
    b_row = jnp.sum(jnp.where(r <= s, lf_col, 0.0), axis=0, keepdims=True)
    m_prev = m_sc[:, :1]
    a_col = b_col + m_prev
    d = jnp.where(tri, b_col - b_row + li, -jnp.inf)
    mt = jnp.maximum(a_col, jnp.max(d, axis=1, keepdims=True))
    w_prev = jnp.exp(a_col - mt)
    sm = lax.dot_general(q, k, NT, preferred_element_type=f32) * jnp.exp(d - mt)
    qc = lax.dot_general(q, c_sc[...].astype(bf16), NT, preferred_element_type=f32)
    num = jnp.dot(sm.astype(bf16), v, preferred_element_type=f32) + w_prev * qc
    qn = jnp.sum(q.astype(f32) * n_sc[...], axis=1, keepdims=True)
    den = jnp.sum(sm, axis=1, keepdims=True) + w_prev * qn
    hh = num / jnp.maximum(jnp.abs(den), jnp.exp(-mt))

    b_end = jnp.sum(lf, axis=1, keepdims=True)
    a_end = b_end + m_prev
    m_new = jnp.maximum(a_end, jnp.max(b_end - b_row + li, axis=1, keepdims=True))
    w_old = jnp.exp(a_end - m_new)
    w_tok = jnp.exp(b_end - b_col + li_col - m_new)
    vw = (v.astype(f32) * w_tok).astype(bf16)
    c_sc[...] = w_old * c_sc[...] + lax.dot_general(vw, k, TN, preferred_element_type=f32)
    n_sc[...] = w_old * n_sc[...] + jnp.sum(k.astype(f32) * w_tok, axis=0, keepdims=True)
    m_sc[...] = jnp.broadcast_to(m_new, m_sc.shape)

    hn = hh * lax.rsqrt(jnp.mean(hh * hh, axis=1, keepdims=True) + EPS) * hg_ref[...]
    zg = zg_ref[...].astype(f32)
    h_ref[...] = (hn * jax.nn.sigmoid(og_ref[...].astype(f32)) * (zg * jax.nn.sigmoid(zg))).astype(h_ref.dtype)

    @pl.when(c == pl.num_programs(2) - 1)
    def _():
        c_out[...] = c_sc[...]
        n_out[...] = n_sc[...]
        m_out[...] = m_sc[...]


def _mlstm(name, mqk, prev, conv_w, conv_b, u5, gi, gf, c0, n0, m0, hg, *, bsz, heads, chunk, row0, out_dtype):
    nc = gi.shape[2]
    dk = mqk.shape[1] // (2 * heads)
    dv = c0.shape[2]
    taps = conv_w.shape[0]
    rb = row0 // chunk
    shared = c0.shape[0] == 1
    shared_prev = prev.shape[0] == 1
    rows = bsz * nc * chunk

    def rmap(cb):
        return lambda b, h, c: (rb + b * nc + c, cb(h))

    def smap(b, h, c):
        return (0 if shared else b, h, 0, 0)

    def pmap(cb):
        return lambda b, h, c: (0 if shared_prev else b, 0, cb(h))

    gate = pl.BlockSpec((None, None, None, 1, chunk), lambda b, h, c: (b, h, c, 0, 0))
    kern = functools.partial(_mlstm_kernel, heads=heads, shared_init=shared, q_scale=dk ** -0.5)
    return pl.pallas_call(
        kern,
        grid=(bsz, heads, nc),
        in_specs=[pl.BlockSpec(memory_space=pltpu.SMEM),
                  pl.BlockSpec((chunk, dk), rmap(lambda h: h)),
                  pl.BlockSpec((chunk, dk), rmap(lambda h: heads + h)),
                  pl.BlockSpec((None, taps - 1, dk), pmap(lambda h: h)),
                  pl.BlockSpec((None, taps - 1, dk), pmap(lambda h: heads + h)),
                  pl.BlockSpec((taps, dk), lambda b, h, c: (0, h)),
                  pl.BlockSpec((taps, dk), lambda b, h, c: (0, heads + h)),
                  pl.BlockSpec((1, dk), lambda b, h, c: (0, h)),
                  pl.BlockSpec((1, dk), lambda b, h, c: (0, heads + h)),
                  pl.BlockSpec((chunk, dv), rmap(lambda h: h)),
                  gate, gate,
                  pl.BlockSpec((None, None, dv, dk), smap),
                  pl.BlockSpec((None, None, 1, dk), smap),
                  pl.BlockSpec((chunk, dv), rmap(lambda h: heads + h)),
                  pl.BlockSpec((chunk, dv), rmap(lambda h: 2 * heads + h)),
                  pl.BlockSpec((1, dv), lambda b, h, c: (0, h))],
        out_specs=[pl.BlockSpec((chunk, dv), lambda b, h, c: (b * nc + c, h)),
                   pl.BlockSpec((None, None, dv, dk), lambda b, h, c: (b, h, 0, 0)),
                   pl.BlockSpec((None, None, 1, dk), lambda b, h, c: (b, h, 0, 0)),
                   pl.BlockSpec((None, None, 1, LANES), lambda b, h, c: (b, h, 0, 0))],
        out_shape=[jax.ShapeDtypeStruct((rows, heads * dv), out_dtype),
                   jax.ShapeDtypeStruct((bsz, heads, dv, dk), f32),
                   jax.ShapeDtypeStruct((bsz, heads, 1, dk), f32),
                   jax.ShapeDtypeStruct((bsz, heads, 1, LANES), f32)],
        scratch_shapes=[pltpu.VMEM((chunk + SUBLANES, dk), f32), pltpu.VMEM((chunk + SUBLANES, dk), f32),
                        pltpu.VMEM((dv, dk), f32), pltpu.VMEM((1, dk), f32), pltpu.VMEM((1, LANES), f32)],
        compiler_params=_params("arbitrary", "arbitrary", "arbitrary"),
        name=name,
    )(m0, mqk, mqk, prev, prev, conv_w, conv_w, conv_b, conv_b, u5, gi, gf, c0, n0, u5, u5, hg)


def _rms(x, g):
    return x * lax.rsqrt(jnp.mean(x * x, axis=-1, keepdims=True) + EPS) * g


def _gates(g, b_i, b_f, heads, bsz, chunk):
    gi = g[:, :heads] + b_i
    gf = jax.nn.log_sigmoid(g[:, heads:2 * heads] + b_f)

    def lay(a):
        a = a.reshape(bsz, -1, heads).transpose(0, 2, 1)
        return a.reshape(bsz, heads, -1, 1, chunk)
    return lay(gi), lay(gf)


def kernel(x_prompt, x_sample, cache_k, cache_v, page_table, state_conv, state_C, state_n, state_m, meta, norm_g,
           w_in, b_i, b_f, conv_w, conv_b, lam_q1, lam_k1, lam_q2, lam_k2, subln_g, head_g, w_pa, w_pm, w_out,
           norm_f):
    bsz, seq, d = x_prompt.shape
    dbsz, t_new, _ = x_sample.shape
    depth, n_phys, page, h_a, hd_a = cache_k.shape
    dh_a = hd_a // 2
    w_a = h_a * hd_a
    h_m, dv_m, dk_m = state_C.shape[2], state_C.shape[3], state_C.shape[4]
    w_qk, w_m = h_m * dk_m, h_m * dv_m
    n_meta = meta.shape[0]
    taps = conv_w.shape[1]
    n_s = dbsz * t_new
    mr = bsz * seq

    o_q, o_kv, o_z, o_mqk, o_u5 = 0, w_a, 3 * w_a, 4 * w_a, 4 * w_a + 2 * w_qk
    c_gate = o_u5 + 3 * w_m
    w_gate = jnp.pad(w_in[:, :, c_gate:c_gate + 2 * h_m], ((0, 0), (0, 0), (0, LANES - 2 * h_m)))
    w_g = w_in[:, :, c_gate + 2 * h_m:]
    wpa, wpm, wo = w_pa.astype(bf16), w_pm.astype(bf16), w_out.astype(bf16)
    g_next = jnp.concatenate([norm_g[1:], norm_f[None]], axis=0).reshape(depth, 1, d)
    conv_b2 = conv_b.reshape(depth, 1, 2 * w_qk)

    groups = h_a // SUBLANES
    ck = cache_k.reshape(depth, n_phys, page, groups, SUBLANES, hd_a)
    cv = cache_v.reshape(depth, n_phys, page, groups, SUBLANES, hd_a)

    x_m = x_prompt.reshape(mr, d)
    x_s = jnp.concatenate([x_sample.reshape(n_s, d), meta.astype(f32)], axis=0)
    xn_m = _rms(x_m, norm_g[0]).astype(bf16)
    xn_s = _rms(x_s, norm_g[0]).astype(bf16)
    chunk = 256 if seq % 256 == 0 else seq
    tq = 256 if seq % 256 == 0 else seq
    qscale = dh_a ** -0.5

    def in_proj(tag, xn, l, wide, tm):
        lo = f32 if wide else bf16
        mm = functools.partial(_matmul, layer=l, tm=tm, tn=1024)
        q, = mm(f"{tag}_q", xn, w_in, col0=o_q, ncols=w_a, out_dtypes=[lo],
                scale=qscale if wide else qscale * LOG2E)
        kv = mm(f"{tag}_kv", xn, w_in, col0=o_kv, ncols=2 * w_a, out_dtypes=[f32] if wide else [f32, bf16])
        z, = mm(f"{tag}_z", xn, w_in, col0=o_z, ncols=w_a, out_dtypes=[lo])
        mqk, = mm(f"{tag}_mqk", xn, w_in, col0=o_mqk, ncols=2 * w_qk, out_dtypes=[f32])
        u5, = mm(f"{tag}_u5", xn, w_in, col0=o_u5, ncols=3 * w_m, out_dtypes=[lo])
        g, = mm(f"{tag}_g", xn, w_g, col0=0, ncols=2 * d, out_dtypes=[lo])
        gt, = _matmul(f"{tag}_gate", xn, w_gate, l, 0, LANES, [f32], tm, LANES)
        return q, kv, z, mqk, u5, g, gt

    outs = {k: [] for k in ("kp", "vp", "ks", "vs", "cp", "cs", "Cp", "np", "mp", "Cs", "ns", "ms")}
    for l in range(depth):
        lam_init = 0.8 - 0.6 * math.exp(-0.3 * l)
        lam = (jnp.exp(jnp.sum(lam_q1[l] * lam_k1[l])) - jnp.exp(jnp.sum(lam_q2[l] * lam_k2[l]))
               + lam_init).reshape(1).astype(f32)
        out_scale = 1.0 - lam_init
        sg = subln_g[l].reshape(1, hd_a)
        hg = head_g[l].reshape(1, w_m)
        xn_dtype = bf16 if l + 1 < depth else f32

        q_s, (kv_s,), z_s, mqk_s, u5_s, g_s, gt_s = in_proj(f"s{l}", xn_s, l, True, 1024)
        mqk_samp = mqk_s[:n_s].reshape(dbsz, t_new, 2 * w_qk)
        outs["cs"].append(jnp.concatenate([state_conv[l], mqk_samp], axis=1)[:, -(taps - 1):])
        gi_samp, gf_samp = _gates(gt_s[:n_s], b_i[l], b_f[l], h_m, dbsz, t_new)
        gi_meta, gf_meta = _gates(gt_s[n_s:], b_i[l], b_f[l], h_m, 1, n_meta)
        hm_samp, c_samp, n_samp, m_samp = _mlstm(
            f"s{l}_mlstm_sample", mqk_s, state_conv[l], conv_w[l], conv_b2[l], u5_s, gi_samp, gf_samp, state_C[l],
            state_n[l].reshape(dbsz, h_m, 1, dk_m), state_m[l].reshape(-1), hg,
            bsz=dbsz, heads=h_m, chunk=t_new, row0=0, out_dtype=f32)
        hm_meta, c_meta, n_meta_s, m_meta = _mlstm(
            f"s{l}_mlstm_meta", mqk_s, jnp.zeros((1, taps - 1, 2 * w_qk), f32), conv_w[l], conv_b2[l], u5_s,
            gi_meta, gf_meta, jnp.zeros((1, h_m, dv_m, dk_m), f32), jnp.zeros((1, h_m, 1, dk_m), f32),
            jnp.zeros((h_m,), f32), hg, bsz=1, heads=h_m, chunk=n_meta, row0=n_s, out_dtype=f32)
        outs["Cs"].append(c_samp)
        outs["ns"].append(n_samp.reshape(dbsz, h_m, dk_m))
        outs["ms"].append(m_samp[:, :, 0, 0])

        qs = q_s[:n_s].reshape(dbsz, t_new, h_a, hd_a).transpose(0, 2, 1, 3)
        lane = jnp.arange(hd_a)
        qs = jnp.concatenate([jnp.where(lane < dh_a, qs, 0.0), jnp.where(lane >= dh_a, qs, 0.0)], axis=2).astype(bf16)

        def new_page(a):
            a = a.reshape(dbsz, t_new, groups, SUBLANES, hd_a)
            return jnp.pad(a, ((0, 0), (0, page - t_new), (0, 0), (0, 0), (0, 0)))
        att_samp = _paged_attention(f"s{l}_paged_attn", page_table, lam, qs, ck, cv, new_page(kv_s[:n_s, :w_a]),
                                    new_page(kv_s[:n_s, w_a:]), z_s, sg, layer=l, t_new=t_new, pages_per_step=4,
                                    out_scale=out_scale)
        att_meta = _meta_attention(f"s{l}_meta_attn", lam, q_s, kv_s, z_s, sg, heads=h_a, n_meta=n_meta, row0=n_s,
                                   out_scale=out_scale)
        att_s = jnp.concatenate([att_samp, att_meta], axis=0)
        hm_s = jnp.concatenate([hm_samp, hm_meta], axis=0)
        x_s, xn_s = _out_proj(f"s{l}_out", att_s, hm_s, g_s, x_s, wpa, wpm, wo, g_next[l], l, 256, xn_dtype)
        outs["ks"].append(kv_s[:n_s, :w_a].reshape(dbsz, t_new, h_a, hd_a))
        outs["vs"].append(kv_s[:n_s, w_a:].reshape(dbsz, t_new, h_a, hd_a))

        q_m, (kv_m, kv16_m), z_m, mqk_m, u5_m, g_m, gt_m = in_proj(f"m{l}", xn_m, l, False, 1024)
        outs["cp"].append(mqk_m.reshape(bsz, seq, 2 * w_qk)[:, -(taps - 1):])
        gi_m, gf_m = _gates(gt_m, b_i[l], b_f[l], h_m, bsz, chunk)
        prev_m = mqk_s[n_s + n_meta - (taps - 1):].reshape(1, taps - 1, 2 * w_qk)
        hm_m, c_p, n_p, m_p = _mlstm(
            f"m{l}_mlstm", mqk_m, prev_m, conv_w[l], conv_b2[l], u5_m, gi_m, gf_m, c_meta, n_meta_s,
            m_meta[:, :, 0, 0].reshape(-1), hg, bsz=bsz, heads=h_m, chunk=chunk, row0=0, out_dtype=bf16)
        outs["Cp"].append(c_p)
        outs["np"].append(n_p.reshape(bsz, h_m, dk_m))
        outs["mp"].append(m_p[:, :, 0, 0])

        km = kv_s[n_s:, :w_a].astype(bf16)
        vmt = kv_s[n_s:, w_a:].T.astype(bf16)
        att_m = _flash_attention(f"m{l}_flash_attn", lam, q_m, kv16_m, km, vmt, z_m, sg, bsz=bsz, heads=h_a, tq=tq,
                                 hb=4, out_scale=out_scale)
        x_m, xn_m = _out_proj(f"m{l}_out", att_m, hm_m, g_m, x_m, wpa, wpm, wo, g_next[l], l, 256, xn_dtype)

        def with_meta(real, meta_rows):
            real = real.reshape(bsz, seq, h_a, hd_a)
            meta_rows = jnp.broadcast_to(meta_rows.reshape(1, n_meta, h_a, hd_a), (bsz, n_meta, h_a, hd_a))
            return jnp.concatenate([meta_rows, real], axis=1)
        outs["kp"].append(with_meta(kv_m[:, :w_a], kv_s[n_s:, :w_a]))
        outs["vp"].append(with_meta(kv_m[:, w_a:], kv_s[n_s:, w_a:]))

    y_prompt = xn_m.reshape(bsz, seq, d)
    y_sample = xn_s[:n_s].reshape(dbsz, t_new, d)
    st = {k: jnp.stack(v) for k, v in outs.items()}
    return (y_prompt, y_sample, st["kp"], st["vp"], st["ks"], st["vs"], st["cp"], st["cs"],
            st["Cp"], st["np"], st["mp"], st["Cs"], st["ns"], st["ms"])
```

```python
import functools
import math

import jax
import jax.numpy as jnp
from jax import lax
from jax.experimental import pallas as pl
from jax.experimental.pallas import tpu as pltpu

f32 = jnp.float32
bf16 = jnp.bfloat16

EPS = 1e-6
LANES = 128
SUBLANES = 8
ONES_ROWS = 16
LOG2E = math.log2(math.e)
V7X_VMEM_LIMIT = 56 * 1024 * 1024
NT = (((1,), (1,)), ((), ()))
TN = (((0,), (0,)), ((), ()))


def _params(*sem):
    return pltpu.CompilerParams(dimension_semantics=sem, vmem_limit_bytes=V7X_VMEM_LIMIT)


def _mm_kernel(x_ref, xs_ref, w_ref, *refs, scale, side_scale):
    o_refs, os_ref, wb_sc = refs[:-2], refs[-2], refs[-1]

    @pl.when(pl.program_id(1) == 0)
    def _():
        wb_sc[...] = w_ref[...].astype(bf16)
        os_ref[...] = jnp.dot(xs_ref[...], wb_sc[...], preferred_element_type=f32) * side_scale

    acc = jnp.dot(x_ref[...], wb_sc[...], preferred_element_type=f32)
    if scale != 1.0:
        acc = acc * scale
    for o_ref in o_refs:
        o_ref[...] = acc.astype(o_ref.dtype)


def _matmul(name, x, xs, w, layer, col0, ncols, out_dtypes, tm, tn, scale=1.0, side_scale=1.0):
    m, k = x.shape
    ms = xs.shape[0]
    tm = min(tm, m)
    tn = min(tn, ncols)
    assert col0 % tn == 0 and ncols % tn == 0 and m % tm == 0
    res = pl.pallas_call(
        functools.partial(_mm_kernel, scale=scale, side_scale=side_scale),
        grid=(ncols // tn, m // tm),
        in_specs=[pl.BlockSpec((tm, k), lambda j, i: (i, 0)),
                  pl.BlockSpec((ms, k), lambda j, i: (0, 0)),
                  pl.BlockSpec((None, k, tn), lambda j, i: (layer, 0, col0 // tn + j))],
        out_specs=([pl.BlockSpec((tm, tn), lambda j, i: (i, j)) for _ in out_dtypes]
                   + [pl.BlockSpec((ms, tn), lambda j, i: (0, j))]),
        out_shape=([jax.ShapeDtypeStruct((m, ncols), dt) for dt in out_dtypes]
                   + [jax.ShapeDtypeStruct((ms, ncols), f32)]),
        scratch_shapes=[pltpu.VMEM((k, tn), bf16)],
        compiler_params=_params("arbitrary", "arbitrary"),
        name=name,
    )(x, xs, w)
    return res[:-1], res[-1]


def _kv_kernel(x_ref, xs_ref, w_ref, *refs, nt, shift, meta_row0, aliased):
    if aliased:
        refs = refs[1:]
    o16_ref, out_ref, os_ref, wb_sc, meta_sc, carry_sc = refs
    b, i = pl.program_id(1), pl.program_id(2)
    tm = x_ref.shape[0]

    @pl.when((b == 0) & (i == 0))
    def _():
        wb_sc[...] = w_ref[...].astype(bf16)
        side = jnp.dot(xs_ref[...], wb_sc[...], preferred_element_type=f32)
        os_ref[...] = side
        meta_sc[...] = side[meta_row0:meta_row0 + shift]

    @pl.when(i == 0)
    def _():
        carry_sc[...] = meta_sc[...]

    @pl.when(i < nt)
    def _():
        res = jnp.dot(x_ref[...], wb_sc[...], preferred_element_type=f32)
        o16_ref[...] = res.astype(o16_ref.dtype)
        out_ref[:shift, :] = carry_sc[...]
        out_ref[shift:, :] = res[:tm - shift]
        carry_sc[...] = res[tm - shift:]

    @pl.when(i == nt)
    def _():
        out_ref[:shift, :] = carry_sc[...]
        out_ref[shift:, :] = jnp.zeros((tm - shift, out_ref.shape[1]), f32)


def _kv_proj(name, x, xs, w, layer, col0, ncols, buf, *, depth, bsz, n_meta, meta_row0, tm, tn):
    m, k = x.shape
    ms = xs.shape[0]
    seq = m // bsz
    tm = min(tm, seq)
    nt = seq // tm
    assert seq % tm == 0 and col0 % tn == 0 and ncols % tn == 0 and n_meta % SUBLANES == 0 and n_meta < tm
    aliased = buf is not None
    kern = functools.partial(_kv_kernel, nt=nt, shift=n_meta, meta_row0=meta_row0, aliased=aliased)

    def xmap(j, b, i):
        return (b * nt + jnp.minimum(i, nt - 1), 0)

    in_specs = [pl.BlockSpec((tm, k), xmap),
                pl.BlockSpec((ms, k), lambda j, b, i: (0, 0)),
                pl.BlockSpec((None, k, tn), lambda j, b, i: (layer, 0, col0 // tn + j))]
    args = [x, xs, w]
    if aliased:
        in_specs.append(pl.BlockSpec(memory_space=pl.ANY))
        args.append(buf)
    o16, out, side = pl.pallas_call(
        kern,
        grid=(ncols // tn, bsz, nt + 1),
        in_specs=in_specs,
        out_specs=[pl.BlockSpec((tm, tn), lambda j, b, i: (b * nt + jnp.minimum(i, nt - 1), j)),
                   pl.BlockSpec((None, None, tm, tn), lambda j, b, i: (layer, b, i, j)),
                   pl.BlockSpec((ms, tn), lambda j, b, i: (0, j))],
        out_shape=[jax.ShapeDtypeStruct((m, ncols), bf16),
                   jax.ShapeDtypeStruct((depth, bsz, n_meta + seq, ncols), f32),
                   jax.ShapeDtypeStruct((ms, ncols), f32)],
        scratch_shapes=[pltpu.VMEM((k, tn), bf16), pltpu.VMEM((n_meta, tn), f32), pltpu.VMEM((n_meta, tn), f32)],
        input_output_aliases={3: 1} if aliased else {},
        compiler_params=_params("arbitrary", "arbitrary", "arbitrary"),
        name=name,
    )(*args)
    return o16, out, side


def _out_kernel(a_ref, m_ref, ga_ref, gm_ref, x_ref, wpa_ref, wpm_ref, wo_ref, gn_ref, o_ref, xn_ref):
    ya = jnp.dot(a_ref[...].astype(bf16), wpa_ref[...], preferred_element_type=f32)
    ym = jnp.dot(m_ref[...].astype(bf16), wpm_ref[...], preferred_element_type=f32)
    merged = (jax.nn.sigmoid(ga_ref[...].astype(f32)) * ya
              + jax.nn.sigmoid(gm_ref[...].astype(f32)) * ym)
    x_new = x_ref[...] + jnp.dot(merged.astype(bf16), wo_ref[...], preferred_element_type=f32)
    o_ref[...] = x_new
    xn = x_new * lax.rsqrt(jnp.mean(x_new * x_new, axis=-1, keepdims=True) + EPS) * gn_ref[...]
    xn_ref[...] = xn.astype(xn_ref.dtype)


def _out_proj(name, a, mm, g, x, w_pa, w_pm, w_out, g_next, layer, tm, xn_dtype):
    m, d = x.shape
    tm = min(tm, m)
    wspec = pl.BlockSpec((None, d, d), lambda i: (layer, 0, 0), pipeline_mode=pl.Buffered(1))
    row = pl.BlockSpec((tm, d), lambda i: (i, 0))
    return pl.pallas_call(
        _out_kernel,
        grid=(pl.cdiv(m, tm),),
        in_specs=[row, row, row, pl.BlockSpec((tm, d), lambda i: (i, 1)), row, wspec, wspec, wspec,
                  pl.BlockSpec((1, d), lambda i: (0, 0))],
        out_specs=[row, row],
        out_shape=[jax.ShapeDtypeStruct((m, d), f32), jax.ShapeDtypeStruct((m, d), xn_dtype)],
        compiler_params=_params("arbitrary"),
        name=name,
    )(a, mm, g, g, x, w_pa, w_pm, w_out, g_next)


def _subln_gate(o, z, g, scale):
    y = o * lax.rsqrt(jnp.mean(o * o, axis=-1, keepdims=True) + EPS) * (g * scale)
    zf = z.astype(f32)
    return y * (zf * jax.nn.sigmoid(zf))


def _stack_maps(q, dh):
    lane = lax.broadcasted_iota(jnp.int32, q.shape, 1)
    zero = jnp.zeros_like(q)
    return jnp.concatenate([jnp.where(lane < dh, q, zero), jnp.where(lane >= dh, q, zero)], axis=0)


def _flash_kernel(lam_ref, q_ref, k_ref, v_ref, km_ref, vmt_ref, z_ref, g_ref, o_ref, qs_sc, vt_sc, m_sc, acc_sc,
                  *, tq, hd, hb, out_scale):
    i = pl.program_id(2)
    tr = k_ref.shape[0]
    dh = hd // 2
    ones = jnp.ones((ONES_ROWS, tq), bf16)

    @pl.when(i == 0)
    def _():
        for hh in range(hb):
            for c in range(tr // tq):
                vt = v_ref[c * tq:(c + 1) * tq, hh * hd:(hh + 1) * hd].astype(f32).T.astype(bf16)
                vt_sc[hh, :, c * tq:(c + 1) * tq] = jnp.concatenate([vt, ones], axis=0)

    def scores_t(hh, kb):
        return lax.dot_general(kb, qs_sc[hh], NT, preferred_element_type=f32)

    n_meta = km_ref.shape[0]
    for hh in range(hb):
        cols = slice(hh * hd, (hh + 1) * hd)
        qs_sc[hh] = _stack_maps(q_ref[:, cols], dh)
        s = scores_t(hh, km_ref[:, cols])
        m = jnp.max(s, axis=0, keepdims=True)
        m_sc[hh] = m
        vmt = jnp.concatenate([vmt_ref[cols, :], jnp.ones((ONES_ROWS, n_meta), bf16)], axis=0)
        acc_sc[hh] = jnp.dot(vmt, jnp.exp2(s - m).astype(bf16), preferred_element_type=f32)

    def update(off, visible=None):
        hs = range(hb)
        ss = [scores_t(hh, k_ref[pl.ds(off, tq), hh * hd:(hh + 1) * hd]) for hh in hs]
        if visible is not None:
            ss = [jnp.where(visible, s, -jnp.inf) for s in ss]
        m_prev = [m_sc[hh] for hh in hs]
        m_new = [jnp.maximum(m_prev[hh], jnp.max(ss[hh], axis=0, keepdims=True)) for hh in hs]
        ps = [jnp.exp2(ss[hh] - m_new[hh]).astype(bf16) for hh in hs]
        pvs = [jnp.dot(vt_sc[hh, :, pl.ds(off, tq)], ps[hh], preferred_element_type=f32) for hh in hs]
        for hh in hs:
            acc_sc[hh] = jnp.exp2(m_prev[hh] - m_new[hh]) * acc_sc[hh] + pvs[hh]
            m_sc[hh] = m_new[hh]

    def body(j, carry):
        update(pl.multiple_of(j * tq, tq))
        return carry

    lax.fori_loop(0, i, body, 0)
    key = lax.broadcasted_iota(jnp.int32, (tq, 2 * tq), 0)
    qry = lax.broadcasted_iota(jnp.int32, (tq, 2 * tq), 1)
    update(pl.multiple_of(i * tq, tq), key <= jnp.where(qry >= tq, qry - tq, qry))
    for hh in range(hb):
        cols = slice(hh * hd, (hh + 1) * hd)
        acc = acc_sc[hh]
        o = acc[:hd] * (1.0 / acc[hd:hd + 1])
        out = (o[:, :tq] - lam_ref[0] * o[:, tq:]).T
        o_ref[:, cols] = _subln_gate(out, z_ref[:, cols], g_ref[...], out_scale).astype(o_ref.dtype)


def _flash_attention(name, lam, q, k, v, km, vmt, z, g, *, bsz, heads, tq, hb, out_scale):
    mr, w = q.shape
    tr = mr // bsz
    nq = tr // tq
    hd = w // heads
    n_meta = km.shape[0]
    ng = heads // hb
    kern = functools.partial(_flash_kernel, tq=tq, hd=hd, hb=hb, out_scale=out_scale)
    qspec = pl.BlockSpec((tq, hb * hd), lambda b, h, i: (b * nq + i, h))
    return pl.pallas_call(
        kern,
        grid=(bsz, ng, nq),
        in_specs=[pl.BlockSpec(memory_space=pltpu.SMEM),
                  qspec,
                  pl.BlockSpec((tr, hb * hd), lambda b, h, i: (b, h)),
                  pl.BlockSpec((tr, hb * hd), lambda b, h, i: (b, h)),
                  pl.BlockSpec((n_meta, hb * hd), lambda b, h, i: (0, h)),
                  pl.BlockSpec((hb * hd, n_meta), lambda b, h, i: (h, 0)),
                  qspec,
                  pl.BlockSpec((1, hd), lambda b, h, i: (0, 0))],
        out_specs=qspec,
        out_shape=jax.ShapeDtypeStruct((mr, w), bf16),
        scratch_shapes=[pltpu.VMEM((hb, 2 * tq, hd), bf16), pltpu.VMEM((hb, hd + ONES_ROWS, tr), bf16),
                        pltpu.VMEM((hb, 1, 2 * tq), f32), pltpu.VMEM((hb, hd + ONES_ROWS, 2 * tq), f32)],
        compiler_params=_params("arbitrary", "arbitrary", "arbitrary"),
        name=name,
    )(lam, q, k, v, km, vmt, z, g)


def _meta_attn_kernel(lam_ref, q_ref, k_ref, v_ref, z_ref, g_ref, o_ref, *, dh, out_scale):
    r = q_ref.shape[0]
    qs = _stack_maps(q_ref[...].astype(bf16), dh)
    s = lax.dot_general(qs, k_ref[...].astype(bf16), NT, preferred_element_type=f32)
    row = lax.broadcasted_iota(jnp.int32, s.shape, 0)
    row = jnp.where(row >= r, row - r, row)
    col = lax.broadcasted_iota(jnp.int32, s.shape, 1)
    s = jnp.where(col <= row, s, -jnp.inf)
    p = jnp.exp(s - jnp.max(s, axis=-1, keepdims=True))
    l = jnp.sum(p, axis=-1, keepdims=True)
    o = jnp.dot(p.astype(bf16), v_ref[...].astype(bf16), preferred_element_type=f32) * (1.0 / l)
    out = o[:r] - lam_ref[0] * o[r:]
    o_ref[...] = _subln_gate(out, z_ref[...], g_ref[...], out_scale).astype(o_ref.dtype)


def _meta_attention(name, lam, q, k, v, z, g, *, heads, n_meta, row0, out_scale):
    hd = q.shape[1] // heads
    rb = row0 // n_meta
    kern = functools.partial(_meta_attn_kernel, dh=hd // 2, out_scale=out_scale)
    blk = pl.BlockSpec((n_meta, hd), lambda h: (rb, h))
    return pl.pallas_call(
        kern,
        grid=(heads,),
        in_specs=[pl.BlockSpec(memory_space=pltpu.SMEM), blk, blk, blk, blk,
                  pl.BlockSpec((1, hd), lambda h: (0, 0))],
        out_specs=pl.BlockSpec((n_meta, hd), lambda h: (0, h)),
        out_shape=jax.ShapeDtypeStruct((n_meta, q.shape[1]), f32),
        compiler_params=_params("arbitrary"),
        name=name,
    )(lam, q, k, v, z, g)


def _rows_of_head(ref, hh):
    ntok, _, hd = ref.shape
    return ref.reshape(ntok * SUBLANES, hd)[pl.ds(hh, ntok, stride=SUBLANES), :]


def _paged_kernel(pt_ref, lam_ref, qs_ref, *refs, n_steps, pages_per_step, groups, t_new, out_scale):
    del pt_ref
    npg = pages_per_step * groups
    k_refs, v_refs = refs[:npg], refs[npg:2 * npg]
    kn_refs, vn_refs = refs[2 * npg:2 * npg + groups], refs[2 * npg + groups:2 * npg + 2 * groups]
    z_ref, g_ref, o_ref, m_sc, l_sc, acc_sc = refs[2 * npg + 2 * groups:]
    step = pl.program_id(1)
    rows = 2 * t_new
    heads = groups * SUBLANES
    hd = qs_ref.shape[-1]

    @pl.when(step == 0)
    def _():
        m_sc[...] = jnp.full(m_sc.shape, -jnp.inf, f32)
        l_sc[...] = jnp.zeros(l_sc.shape, f32)
        acc_sc[...] = jnp.zeros(acc_sc.shape, f32)

    def head_rows(ref_list, h):
        return _rows_of_head(ref_list[h // SUBLANES], h % SUBLANES).astype(bf16)

    def process(kpages, vpages, causal):
        ntok = kpages[0][0].shape[0]
        s = jnp.concatenate(
            [jnp.concatenate([lax.dot_general(qs_ref[h], head_rows(kg, h), NT, preferred_element_type=f32)
                              for kg in kpages], axis=1) for h in range(heads)], axis=0)
        if causal:
            t = lax.broadcasted_iota(jnp.int32, s.shape, 0) % t_new
            col = lax.broadcasted_iota(jnp.int32, s.shape, 1)
            s = jnp.where(col <= t, s, -jnp.inf)
        m_prev = m_sc[...]
        m_new = jnp.maximum(m_prev, jnp.max(s, axis=-1, keepdims=True))
        alpha = jnp.exp(m_prev - m_new)
        p = jnp.exp(s - m_new)
        l_sc[...] = alpha * l_sc[...] + jnp.sum(p, axis=-1, keepdims=True)
        m_sc[...] = m_new
        pb = p.astype(bf16)
        pv = jnp.concatenate(
            [sum(jnp.dot(pb[h * rows:(h + 1) * rows, j * ntok:(j + 1) * ntok], head_rows(vg, h),
                         preferred_element_type=f32) for j, vg in enumerate(vpages))
             for h in range(heads)], axis=0)
        acc_sc[...] = alpha * acc_sc[...] + pv

    @pl.when(step < n_steps)
    def _():
        process([k_refs[j * groups:(j + 1) * groups] for j in range(pages_per_step)],
                [v_refs[j * groups:(j + 1) * groups] for j in range(pages_per_step)], False)

    @pl.when(step == n_steps)
    def _():
        process([kn_refs], [vn_refs], True)
        o = acc_sc[...] * (1.0 / l_sc[...])
        lam = lam_ref[0]
        for h in range(heads):
            oh = o[h * rows:h * rows + t_new] - lam * o[h * rows + t_new:(h + 1) * rows]
            cols = slice(h * hd, (h + 1) * hd)
            o_ref[:, cols] = _subln_gate(oh, z_ref[:, cols], g_ref[...], out_scale).astype(o_ref.dtype)


def _paged_attention(name, page_table, lam, qs, cache_k, cache_v, kn, vn, z, g, *, layer, t_new, pages_per_step,
                     out_scale):
    dbsz, n_pages = page_table.shape
    pp = pages_per_step
    assert n_pages % pp == 0
    n_steps = n_pages // pp
    page, groups, hd = cache_k.shape[2], cache_k.shape[3], cache_k.shape[5]
    heads = groups * SUBLANES
    w = heads * hd
    rows = 2 * t_new

    def page_spec(j, gidx):
        def imap(b, s, pt):
            return (layer, pt[b, jnp.minimum(s, n_steps - 1) * pp + j], 0, gidx, 0, 0)
        return pl.BlockSpec((None, None, page, None, SUBLANES, hd), imap)

    def new_spec(gidx):
        return pl.BlockSpec((None, kn.shape[1], None, SUBLANES, hd), lambda b, s, pt: (b, 0, gidx, 0, 0))

    page_specs = [page_spec(j, gidx) for j in range(pp) for gidx in range(groups)]
    new_specs = [new_spec(gidx) for gidx in range(groups)]
    kern = functools.partial(_paged_kernel, n_steps=n_steps, pages_per_step=pp, groups=groups, t_new=t_new,
                             out_scale=out_scale)
    grid_spec = pltpu.PrefetchScalarGridSpec(
        num_scalar_prefetch=1,
        grid=(dbsz, n_steps + 1),
        in_specs=([pl.BlockSpec(memory_space=pltpu.SMEM),
                   pl.BlockSpec((None, heads, rows, hd), lambda b, s, pt: (b, 0, 0, 0))]
                  + page_specs + page_specs + new_specs + new_specs
                  + [pl.BlockSpec((t_new, w), lambda b, s, pt: (b, 0)),
                     pl.BlockSpec((1, hd), lambda b, s, pt: (0, 0))]),
        out_specs=pl.BlockSpec((t_new, w), lambda b, s, pt: (b, 0)),
        scratch_shapes=[pltpu.VMEM((heads * rows, 1), f32), pltpu.VMEM((heads * rows, 1), f32),
                        pltpu.VMEM((heads * rows, hd), f32)],
    )
    return pl.pallas_call(
        kern,
        grid_spec=grid_spec,
        out_shape=jax.ShapeDtypeStruct((dbsz * t_new, w), f32),
        compiler_params=_params("arbitrary", "arbitrary"),
        name=name,
    )(page_table, lam, qs, *([cache_k] * (pp * groups)), *([cache_v] * (pp * groups)),
      *([kn] * groups), *([vn] * groups), z, g)


def _mlstm_kernel(m0_ref, uq_ref, uk_ref, pq_ref, pk_ref, wq_ref, wk_ref, bq_ref, bk_ref, v_ref, gi_ref, gf_ref,
                  c0_ref, n0_ref, og_ref, zg_ref, hg_ref, h_ref, c_out, n_out, m_out,
                  padq_sc, padk_sc, c_sc, n_sc, m_sc, *, heads, hb, shared_init, q_scale):
    b, hgrp, c = pl.program_id(0), pl.program_id(1), pl.program_id(2)
    chunk = uq_ref.shape[0]
    taps = wq_ref.shape[0]
    dv, dk = c_sc.shape[1], c_sc.shape[2]
    hs = range(hb)
    base = SUBLANES - (taps - 1)

    @pl.when(c == 0)
    def _():
        c_sc[...] = c0_ref[...]
        n_sc[...] = n0_ref[...]
        for hh in hs:
            m0 = m0_ref[hgrp * hb + hh if shared_init else b * heads + hgrp * hb + hh]
            m_sc[hh] = jnp.full((1, LANES), m0, f32)
        padq_sc[base:SUBLANES, :] = pq_ref[...]
        padk_sc[base:SUBLANES, :] = pk_ref[...]

    def conv_silu(u_ref, w_ref, b_ref, pad_sc):
        pad_sc[SUBLANES:SUBLANES + chunk, :] = u_ref[...]
        out = b_ref[...] + pad_sc[base:base + chunk, :] * w_ref[0:1, :]
        for j in range(1, taps):
            out = out + pad_sc[base + j:base + j + chunk, :] * w_ref[j:j + 1, :]
        pad_sc[base:SUBLANES, :] = pad_sc[base + chunk:SUBLANES + chunk, :]
        return out * jax.nn.sigmoid(out)

    q_all = (conv_silu(uq_ref, wq_ref, bq_ref, padq_sc) * q_scale).astype(bf16)
    k_all = conv_silu(uk_ref, wk_ref, bk_ref, padk_sc).astype(bf16)
    q = [q_all[:, hh * dk:(hh + 1) * dk] for hh in hs]
    k = [k_all[:, hh * dk:(hh + 1) * dk] for hh in hs]
    v = [v_ref[:, hh * dv:(hh + 1) * dv].astype(bf16) for hh in hs]
    c_prev = [c_sc[hh] for hh in hs]
    n_prev = [n_sc[hh] for hh in hs]
    m_prev = [m_sc[hh][:, :1] for hh in hs]
    qk = [lax.dot_general(q[hh], k[hh], NT, preferred_element_type=f32) for hh in hs]
    qc = [lax.dot_general(q[hh], c_prev[hh].astype(bf16), NT, preferred_element_type=f32) for hh in hs]

    li = [gi_ref[hh] for hh in hs]
    lf = [gf_ref[hh] for hh in hs]
    r = lax.broadcasted_iota(jnp.int32, (chunk, chunk), 0)
    s = lax.broadcasted_iota(jnp.int32, (chunk, chunk), 1)
    tri, eye = s <= r, s == r
    lf_col = [jnp.sum(jnp.where(eye, lf[hh], 0.0), axis=1, keepdims=True) for hh in hs]
    li_col = [jnp.sum(jnp.where(eye, li[hh], 0.0), axis=1, keepdims=True) for hh in hs]
    b_col = [jnp.sum(jnp.where(tri, lf[hh], 0.0), axis=1, keepdims=True) for hh in hs]
    b_row = [jnp.sum(jnp.where(r <= s, lf_col[hh], 0.0), axis=0, keepdims=True) for hh in hs]
    a_col = [b_col[hh] + m_prev[hh] for hh in hs]
    d = [jnp.where(tri, b_col[hh] - b_row[hh] + li[hh], -jnp.inf) for hh in hs]
    mt = [jnp.maximum(a_col[hh], jnp.max(d[hh], axis=1, keepdims=True)) for hh in hs]
    w_prev = [jnp.exp(a_col[hh] - mt[hh]) for hh in hs]
    sm = [qk[hh] * jnp.exp(d[hh] - mt[hh]) for hh in hs]
    sv = [jnp.dot(sm[hh].astype(bf16), v[hh], preferred_element_type=f32) for hh in hs]

    b_end = [jnp.sum(lf[hh], axis=1, keepdims=True) for hh in hs]
    a_end = [b_end[hh] + m_prev[hh] for hh in hs]
    m_new = [jnp.maximum(a_end[hh], jnp.max(b_end[hh] - b_row[hh] + li[hh], axis=1, keepdims=True)) for hh in hs]
    w_old = [jnp.exp(a_end[hh] - m_new[hh]) for hh in hs]
    w_tok = [jnp.exp(b_end[hh] - b_col[hh] + li_col[hh] - m_new[hh]) for hh in hs]
    vw = [(v[hh].astype(f32) * w_tok[hh]).astype(bf16) for hh in hs]
    vk = [lax.dot_general(vw[hh], k[hh], TN, preferred_element_type=f32) for hh in hs]
    for hh in hs:
        c_sc[hh] = w_old[hh] * c_prev[hh] + vk[hh]
        n_sc[hh] = w_old[hh] * n_prev[hh] + jnp.sum(k[hh].astype(f32) * w_tok[hh], axis=0, keepdims=True)
        m_sc[hh] = jnp.broadcast_to(m_new[hh], (1, LANES))

    for hh in hs:
        num = sv[hh] + w_prev[hh] * qc[hh]
        qn = jnp.sum(q[hh].astype(f32) * n_prev[hh], axis=1, keepdims=True)
        den = jnp.sum(sm[hh], axis=1, keepdims=True) + w_prev[hh] * qn
        hx = num / jnp.maximum(jnp.abs(den), jnp.exp(-mt[hh]))
        cols = slice(hh * dv, (hh + 1) * dv)
        hn = hx * lax.rsqrt(jnp.mean(hx * hx, axis=1, keepdims=True) + EPS) * hg_ref[:, cols]
        zg = zg_ref[:, cols].astype(f32)
        h_ref[:, cols] = (hn * jax.nn.sigmoid(og_ref[:, cols].astype(f32))
                          * (zg * jax.nn.sigmoid(zg))).astype(h_ref.dtype)

    @pl.when(c == pl.num_programs(2) - 1)
    def _():
        c_out[...] = c_sc[...]
        n_out[...] = n_sc[...]
        m_out[...] = m_sc[...]


def _mlstm(name, mqk, prev, conv_w, conv_b, u5, gi, gf, c0, n0, m0, hg, *, bsz, heads, chunk, row0, out_dtype,
           hb=2):
    nc = gi.shape[2]
    dk = mqk.shape[1] // (2 * heads)
    dv = c0.shape[2]
    taps = conv_w.shape[0]
    rb = row0 // chunk
    shared = c0.shape[0] == 1
    shared_prev = prev.shape[0] == 1
    rows = bsz * nc * chunk

    ng = heads // hb

    def rmap(cb):
        return lambda b, h, c: (rb + b * nc + c, cb(h))

    def smap(b, h, c):
        return (0 if shared else b, h, 0, 0)

    def pmap(cb):
        return lambda b, h, c: (0 if shared_prev else b, 0, cb(h))

    gate = pl.BlockSpec((None, hb, None, 1, chunk), lambda b, h, c: (b, h, c, 0, 0))
    kern = functools.partial(_mlstm_kernel, heads=heads, hb=hb, shared_init=shared, q_scale=dk ** -0.5)
    return pl.pallas_call(
        kern,
        grid=(bsz, ng, nc),
        in_specs=[pl.BlockSpec(memory_space=pltpu.SMEM),
                  pl.BlockSpec((chunk, hb * dk), rmap(lambda h: h)),
                  pl.BlockSpec((chunk, hb * dk), rmap(lambda h: ng + h)),
                  pl.BlockSpec((None, taps - 1, hb * dk), pmap(lambda h: h)),
                  pl.BlockSpec((None, taps - 1, hb * dk), pmap(lambda h: ng + h)),
                  pl.BlockSpec((taps, hb * dk), lambda b, h, c: (0, h)),
                  pl.BlockSpec((taps, hb * dk), lambda b, h, c: (0, ng + h)),
                  pl.BlockSpec((1, hb * dk), lambda b, h, c: (0, h)),
                  pl.BlockSpec((1, hb * dk), lambda b, h, c: (0, ng + h)),
                  pl.BlockSpec((chunk, hb * dv), rmap(lambda h: h)),
                  gate, gate,
                  pl.BlockSpec((None, hb, dv, dk), smap),
                  pl.BlockSpec((None, hb, 1, dk), smap),
                  pl.BlockSpec((chunk, hb * dv), rmap(lambda h: ng + h)),
                  pl.BlockSpec((chunk, hb * dv), rmap(lambda h: 2 * ng + h)),
                  pl.BlockSpec((1, hb * dv), lambda b, h, c: (0, h))],
        out_specs=[pl.BlockSpec((chunk, hb * dv), lambda b, h, c: (b * nc + c, h)),
                   pl.BlockSpec((None, hb, dv, dk), lambda b, h, c: (b, h, 0, 0)),
                   pl.BlockSpec((None, hb, 1, dk), lambda b, h, c: (b, h, 0, 0)),
                   pl.BlockSpec((None, hb, 1, LANES), lambda b, h, c: (b, h, 0, 0))],
        out_shape=[jax.ShapeDtypeStruct((rows, heads * dv), out_dtype),
                   jax.ShapeDtypeStruct((bsz, heads, dv, dk), f32),
                   jax.ShapeDtypeStruct((bsz, heads, 1, dk), f32),
                   jax.ShapeDtypeStruct((bsz, heads, 1, LANES), f32)],
        scratch_shapes=[pltpu.VMEM((chunk + SUBLANES, hb * dk), f32), pltpu.VMEM((chunk + SUBLANES, hb * dk), f32),
                        pltpu.VMEM((hb, dv, dk), f32), pltpu.VMEM((hb, 1, dk), f32), pltpu.VMEM((hb, 1, LANES), f32)],
        compiler_params=_params("arbitrary", "arbitrary", "arbitrary"),
        name=name,
    )(m0, mqk, mqk, prev, prev, conv_w, conv_w, conv_b, conv_b, u5, gi, gf, c0, n0, u5, u5, hg)


def _rms(x, g):
    return x * lax.rsqrt(jnp.mean(x * x, axis=-1, keepdims=True) + EPS) * g


def _gates(g, b_i, b_f, heads, bsz, chunk):
    gi = g[:, :heads] + b_i
    gf = jax.nn.log_sigmoid(g[:, heads:2 * heads] + b_f)

    def lay(a):
        a = a.reshape(bsz, -1, heads).transpose(0, 2, 1)
        return a.reshape(bsz, heads, -1, 1, chunk)
    return lay(gi), lay(gf)


def kernel(x_prompt, x_sample, cache_k, cache_v, page_table, state_conv, state_C, state_n, state_m, meta, norm_g,
           w_in, b_i, b_f, conv_w, conv_b, lam_q1, lam_k1, lam_q2, lam_k2, subln_g, head_g, w_pa, w_pm, w_out,
           norm_f):
    bsz, seq, d = x_prompt.shape
    dbsz, t_new, _ = x_sample.shape
    depth, n_phys, page, h_a, hd_a = cache_k.shape
    dh_a = hd_a // 2
    w_a = h_a * hd_a
    h_m, dv_m, dk_m = state_C.shape[2], state_C.shape[3], state_C.shape[4]
    w_qk, w_m = h_m * dk_m, h_m * dv_m
    n_meta = meta.shape[0]
    taps = conv_w.shape[1]
    n_s = dbsz * t_new
    mr = bsz * seq

    o_q, o_kv, o_z, o_mqk, o_u5 = 0, w_a, 3 * w_a, 4 * w_a, 4 * w_a + 2 * w_qk
    c_gate = o_u5 + 3 * w_m
    w_gate = jnp.pad(w_in[:, :, c_gate:c_gate + 2 * h_m], ((0, 0), (0, 0), (0, LANES - 2 * h_m)))
    w_g = w_in[:, :, c_gate + 2 * h_m:]
    wpa, wpm, wo = w_pa.astype(bf16), w_pm.astype(bf16), w_out.astype(bf16)
    g_next = jnp.concatenate([norm_g[1:], norm_f[None]], axis=0).reshape(depth, 1, d)
    conv_b2 = conv_b.reshape(depth, 1, 2 * w_qk)

    groups = h_a // SUBLANES
    ck = cache_k.reshape(depth, n_phys, page, groups, SUBLANES, hd_a)
    cv = cache_v.reshape(depth, n_phys, page, groups, SUBLANES, hd_a)

    x_m = x_prompt.reshape(mr, d)
    x_s = jnp.concatenate([x_sample.reshape(n_s, d), meta.astype(f32)], axis=0)
    xn_m = _rms(x_m, norm_g[0]).astype(bf16)
    xn_s = _rms(x_s, norm_g[0]).astype(bf16)
    chunk = 256 if seq % 256 == 0 else seq
    tq = 256 if seq % 256 == 0 else seq
    qscale = dh_a ** -0.5

    outs = {k: [] for k in ("ks", "vs", "cp", "cs", "Cp", "np", "mp", "Cs", "ns", "ms")}
    kp_buf = vp_buf = None
    for l in range(depth):
        lam_init = 0.8 - 0.6 * math.exp(-0.3 * l)
        lam = (jnp.exp(jnp.sum(lam_q1[l] * lam_k1[l])) - jnp.exp(jnp.sum(lam_q2[l] * lam_k2[l]))
               + lam_init).reshape(1).astype(f32)
        out_scale = 1.0 - lam_init
        sg = subln_g[l].reshape(1, hd_a)
        hg = head_g[l].reshape(1, w_m)
        xn_dtype = bf16 if l + 1 < depth else f32

        mm = functools.partial(_matmul, layer=l, tm=1024, tn=1024)
        (q_m,), q_s = mm(f"l{l}_q", xn_m, xn_s, w_in, col0=o_q, ncols=w_a, out_dtypes=[bf16],
                         scale=qscale * LOG2E, side_scale=qscale)
        kvp = functools.partial(_kv_proj, depth=depth, bsz=bsz, n_meta=n_meta, meta_row0=n_s, tm=1024, tn=1024)
        k16_m, kp_buf, k_s = kvp(f"l{l}_k", xn_m, xn_s, w_in, l, o_kv, w_a, kp_buf)
        v16_m, vp_buf, v_s = kvp(f"l{l}_v", xn_m, xn_s, w_in, l, o_kv + w_a, w_a, vp_buf)
        (z_m,), z_s = mm(f"l{l}_z", xn_m, xn_s, w_in, col0=o_z, ncols=w_a, out_dtypes=[bf16])
        (mqk_m,), mqk_s = mm(f"l{l}_mqk", xn_m, xn_s, w_in, col0=o_mqk, ncols=2 * w_qk, out_dtypes=[f32])
        (u5_m,), u5_s = mm(f"l{l}_u5", xn_m, xn_s, w_in, col0=o_u5, ncols=3 * w_m, out_dtypes=[bf16])
        (g_m,), g_s = mm(f"l{l}_g", xn_m, xn_s, w_g, col0=0, ncols=2 * d, out_dtypes=[bf16])
        (gt_m,), gt_s = _matmul(f"l{l}_gate", xn_m, xn_s, w_gate, l, 0, LANES, [f32], 1024, LANES)

        mqk_samp = mqk_s[:n_s].reshape(dbsz, t_new, 2 * w_qk)
        outs["cs"].append(jnp.concatenate([state_conv[l], mqk_samp], axis=1)[:, -(taps - 1):])
        gi_samp, gf_samp = _gates(gt_s[:n_s], b_i[l], b_f[l], h_m, dbsz, t_new)
        gi_meta, gf_meta = _gates(gt_s[n_s:], b_i[l], b_f[l], h_m, 1, n_meta)
        hm_samp, c_samp, n_samp, m_samp = _mlstm(
            f"s{l}_mlstm_sample", mqk_s, state_conv[l], conv_w[l], conv_b2[l], u5_s, gi_samp, gf_samp, state_C[l],
            state_n[l].reshape(dbsz, h_m, 1, dk_m), state_m[l].reshape(-1), hg,
            bsz=dbsz, heads=h_m, chunk=t_new, row0=0, out_dtype=f32)
        hm_meta, c_meta, n_meta_s, m_meta = _mlstm(
            f"s{l}_mlstm_meta", mqk_s, jnp.zeros((1, taps - 1, 2 * w_qk), f32), conv_w[l], conv_b2[l], u5_s,
            gi_meta, gf_meta, jnp.zeros((1, h_m, dv_m, dk_m), f32), jnp.zeros((1, h_m, 1, dk_m), f32),
            jnp.zeros((h_m,), f32), hg, bsz=1, heads=h_m, chunk=n_meta, row0=n_s, out_dtype=f32)
        outs["Cs"].append(c_samp)
        outs["ns"].append(n_samp.reshape(dbsz, h_m, dk_m))
        outs["ms"].append(m_samp[:, :, 0, 0])

        qs = q_s[:n_s].reshape(dbsz, t_new, h_a, hd_a).transpose(0, 2, 1, 3)
        lane = jnp.arange(hd_a)
        qs = jnp.concatenate([jnp.where(lane < dh_a, qs, 0.0), jnp.where(lane >= dh_a, qs, 0.0)], axis=2).astype(bf16)

        def new_page(a):
            a = a.reshape(dbsz, t_new, groups, SUBLANES, hd_a)
            return jnp.pad(a, ((0, 0), (0, page - t_new), (0, 0), (0, 0), (0, 0)))
        att_samp = _paged_attention(f"s{l}_paged_attn", page_table, lam, qs, ck, cv, new_page(k_s[:n_s]),
                                    new_page(v_s[:n_s]), z_s, sg, layer=l, t_new=t_new,
                                    pages_per_step=8 if page_table.shape[1] % 8 == 0 else 4,
                                    out_scale=out_scale)
        att_meta = _meta_attention(f"s{l}_meta_attn", lam, q_s, k_s, v_s, z_s, sg, heads=h_a, n_meta=n_meta,
                                   row0=n_s, out_scale=out_scale)
        att_s = jnp.concatenate([att_samp, att_meta], axis=0)
        hm_s = jnp.concatenate([hm_samp, hm_meta], axis=0)
        x_s, xn_s = _out_proj(f"s{l}_out", att_s, hm_s, g_s, x_s, wpa, wpm, wo, g_next[l], l, 256, xn_dtype)
        outs["ks"].append(k_s[:n_s].reshape(dbsz, t_new, h_a, hd_a))
        outs["vs"].append(v_s[:n_s].reshape(dbsz, t_new, h_a, hd_a))

        outs["cp"].append(mqk_m.reshape(bsz, seq, 2 * w_qk)[:, -(taps - 1):])
        gi_m, gf_m = _gates(gt_m, b_i[l], b_f[l], h_m, bsz, chunk)
        prev_m = mqk_s[n_s + n_meta - (taps - 1):].reshape(1, taps - 1, 2 * w_qk)
        hm_m, c_p, n_p, m_p = _mlstm(
            f"m{l}_mlstm", mqk_m, prev_m, conv_w[l], conv_b2[l], u5_m, gi_m, gf_m, c_meta, n_meta_s,
            m_meta[:, :, 0, 0].reshape(-1), hg, bsz=bsz, heads=h_m, chunk=chunk, row0=0, out_dtype=bf16)
        outs["Cp"].append(c_p)
        outs["np"].append(n_p.reshape(bsz, h_m, dk_m))
        outs["mp"].append(m_p[:, :, 0, 0])

        km = k_s[n_s:].astype(bf16)
        vmt = v_s[n_s:].T.astype(bf16)
        att_m = _flash_attention(f"m{l}_flash_attn", lam, q_m, k16_m, v16_m, km, vmt, z_m, sg, bsz=bsz, heads=h_a,
                                 tq=tq, hb=4, out_scale=out_scale)
        x_m, xn_m = _out_proj(f"m{l}_out", att_m, hm_m, g_m, x_m, wpa, wpm, wo, g_next[l], l, 256, xn_dtype)

    y_prompt = xn_m.reshape(bsz, seq, d)
    y_sample = xn_s[:n_s].reshape(dbsz, t_new, d)
    st = {k: jnp.stack(v) for k, v in outs.items()}
    kp = kp_buf.reshape(depth, bsz, n_meta + seq, h_a, hd_a)
    vp = vp_buf.reshape(depth, bsz, n_meta + seq, h_a, hd_a)
    return (y_prompt, y_sample, kp, vp, st["ks"], st["vs"], st["cp"], st["cs"],
            st["Cp"], st["np"], st["mp"], st["Cs"], st["ns"], st["ms"])
```

```python
import functools
import math

import jax
import jax.numpy as jnp
from jax import lax
from jax.experimental import pallas as pl
from jax.experimental.pallas import tpu as pltpu

f32 = jnp.float32
bf16 = jnp.bfloat16

EPS = 1e-6
LANES = 128
SUBLANES = 8
ONES_ROWS = 16
LOG2E = math.log2(math.e)
V7X_VMEM_LIMIT = 56 * 1024 * 1024
NT = (((1,), (1,)), ((), ()))
TN = (((0,), (0,)), ((), ()))


def _params(*sem):
    return pltpu.CompilerParams(dimension_semantics=sem, vmem_limit_bytes=V7X_VMEM_LIMIT)


def _mm_kernel(x_ref, xs_ref, w_ref, *refs, scale, side_scale):
    o_refs, os_ref, wb_sc = refs[:-2], refs[-2], refs[-1]

    @pl.when(pl.program_id(1) == 0)
    def _():
        wb_sc[...] = w_ref[...].astype(bf16)
        os_ref[...] = lax.dot_general(xs_ref[...], wb_sc[...], NT, preferred_element_type=f32) * side_scale

    acc = lax.dot_general(x_ref[...], wb_sc[...], NT, preferred_element_type=f32)
    if scale != 1.0:
        acc = acc * scale
    for o_ref in o_refs:
        o_ref[...] = acc.astype(o_ref.dtype)


def _matmul(name, x, xs, w, layer, col0, ncols, out_dtypes, tm, tn, scale=1.0, side_scale=1.0):
    m, k = x.shape
    ms = xs.shape[0]
    tm = min(tm, m)
    tn = min(tn, ncols)
    assert col0 % tn == 0 and ncols % tn == 0 and m % tm == 0
    res = pl.pallas_call(
        functools.partial(_mm_kernel, scale=scale, side_scale=side_scale),
        grid=(ncols // tn, m // tm),
        in_specs=[pl.BlockSpec((tm, k), lambda j, i: (i, 0)),
                  pl.BlockSpec((ms, k), lambda j, i: (0, 0)),
                  pl.BlockSpec((None, tn, k), lambda j, i: (layer, col0 // tn + j, 0))],
        out_specs=([pl.BlockSpec((tm, tn), lambda j, i: (i, j)) for _ in out_dtypes]
                   + [pl.BlockSpec((ms, tn), lambda j, i: (0, j))]),
        out_shape=([jax.ShapeDtypeStruct((m, ncols), dt) for dt in out_dtypes]
                   + [jax.ShapeDtypeStruct((ms, ncols), f32)]),
        scratch_shapes=[pltpu.VMEM((tn, k), bf16)],
        compiler_params=_params("arbitrary", "arbitrary"),
        name=name,
    )(x, xs, w)
    return res[:-1], res[-1]


def _store_rows_of_head(ref, hh, row0, val):
    ntok, _, hd = ref.shape
    ref.reshape(ntok * SUBLANES, hd)[pl.ds(row0 * SUBLANES + hh, val.shape[0], stride=SUBLANES), :] = val


def _kv_kernel(x_ref, xs_ref, w_ref, *refs, nt, shift, meta_row0, aliased):
    if aliased:
        refs = refs[1:]
    o16_ref, out_ref, os_ref, wb_sc, meta_sc, carry_sc = refs
    b, i = pl.program_id(1), pl.program_id(2)
    tm = x_ref.shape[0]

    @pl.when((b == 0) & (i == 0))
    def _():
        wb_sc[...] = w_ref[...].astype(bf16)
        side = lax.dot_general(xs_ref[...], wb_sc[...], NT, preferred_element_type=f32)
        os_ref[...] = side
        meta_sc[...] = side[meta_row0:meta_row0 + shift]

    @pl.when(i == 0)
    def _():
        carry_sc[...] = meta_sc[...]

    hd = out_ref.shape[2]

    def put(row0, val):
        for hh in range(SUBLANES):
            _store_rows_of_head(out_ref, hh, row0, val[:, hh * hd:(hh + 1) * hd])

    @pl.when(i < nt)
    def _():
        res = lax.dot_general(x_ref[...], wb_sc[...], NT, preferred_element_type=f32)
        o16_ref[...] = res.astype(o16_ref.dtype)
        put(0, carry_sc[...])
        put(shift, res[:tm - shift])
        carry_sc[...] = res[tm - shift:]

    @pl.when(i == nt)
    def _():
        out_ref[shift:] = jnp.zeros((tm - shift,) + out_ref.shape[1:], f32)
        put(0, carry_sc[...])


def _kv_proj(name, x, xs, w, layer, col0, ncols, buf, *, depth, bsz, n_meta, meta_row0, tm, tn):
    m, k = x.shape
    ms = xs.shape[0]
    seq = m // bsz
    tm = min(tm, seq)
    nt = seq // tm
    assert seq % tm == 0 and col0 % tn == 0 and ncols % tn == 0 and n_meta % SUBLANES == 0 and n_meta < tm
    hd = tn // SUBLANES
    assert hd % LANES == 0
    aliased = buf is not None
    kern = functools.partial(_kv_kernel, nt=nt, shift=n_meta, meta_row0=meta_row0, aliased=aliased)

    def xmap(j, b, i):
        return (b * nt + jnp.minimum(i, nt - 1), 0)

    in_specs = [pl.BlockSpec((tm, k), xmap),
                pl.BlockSpec((ms, k), lambda j, b, i: (0, 0)),
                pl.BlockSpec((None, tn, k), lambda j, b, i: (layer, col0 // tn + j, 0))]
    args = [x, xs, w]
    if aliased:
        in_specs.append(pl.BlockSpec(memory_space=pl.ANY))
        args.append(buf)
    o16, out, side = pl.pallas_call(
        kern,
        grid=(ncols // tn, bsz, nt + 1),
        in_specs=in_specs,
        out_specs=[pl.BlockSpec((tm, tn), lambda j, b, i: (b * nt + jnp.minimum(i, nt - 1), j)),
                   pl.BlockSpec((None, None, tm, None, SUBLANES, hd), lambda j, b, i: (layer, b, i, j, 0, 0)),
                   pl.BlockSpec((ms, tn), lambda j, b, i: (0, j))],
        out_shape=[jax.ShapeDtypeStruct((m, ncols), bf16),
                   jax.ShapeDtypeStruct((depth, bsz, n_meta + seq, ncols // tn, SUBLANES, hd), f32),
                   jax.ShapeDtypeStruct((ms, ncols), f32)],
        scratch_shapes=[pltpu.VMEM((tn, k), bf16), pltpu.VMEM((n_meta, tn), f32), pltpu.VMEM((n_meta, tn), f32)],
        input_output_aliases={3: 1} if aliased else {},
        compiler_params=_params("arbitrary", "arbitrary", "arbitrary"),
        name=name,
    )(*args)
    return o16, out, side


def _out_kernel(a_ref, m_ref, ga_ref, gm_ref, x_ref, wpa_ref, wpm_ref, wo_ref, gn_ref, o_ref, xn_ref):
    ya = jnp.dot(a_ref[...].astype(bf16), wpa_ref[...], preferred_element_type=f32)
    ym = jnp.dot(m_ref[...].astype(bf16), wpm_ref[...], preferred_element_type=f32)
    merged = (jax.nn.sigmoid(ga_ref[...].astype(f32)) * ya
              + jax.nn.sigmoid(gm_ref[...].astype(f32)) * ym)
    x_new = x_ref[...] + jnp.dot(merged.astype(bf16), wo_ref[...], preferred_element_type=f32)
    o_ref[...] = x_new
    xn = x_new * lax.rsqrt(jnp.mean(x_new * x_new, axis=-1, keepdims=True) + EPS) * gn_ref[...]
    xn_ref[...] = xn.astype(xn_ref.dtype)


def _out_proj(name, a, mm, g, x, w_pa, w_pm, w_out, g_next, layer, tm, xn_dtype):
    m, d = x.shape
    tm = min(tm, m)
    wspec = pl.BlockSpec((None, d, d), lambda i: (layer, 0, 0), pipeline_mode=pl.Buffered(1))
    row = pl.BlockSpec((tm, d), lambda i: (i, 0))
    return pl.pallas_call(
        _out_kernel,
        grid=(pl.cdiv(m, tm),),
        in_specs=[row, row, row, pl.BlockSpec((tm, d), lambda i: (i, 1)), row, wspec, wspec, wspec,
                  pl.BlockSpec((1, d), lambda i: (0, 0))],
        out_specs=[row, row],
        out_shape=[jax.ShapeDtypeStruct((m, d), f32), jax.ShapeDtypeStruct((m, d), xn_dtype)],
        compiler_params=_params("arbitrary"),
        name=name,
    )(a, mm, g, g, x, w_pa, w_pm, w_out, g_next)


def _subln_gate(o, z, g, scale):
    y = o * lax.rsqrt(jnp.mean(o * o, axis=-1, keepdims=True) + EPS) * (g * scale)
    zf = z.astype(f32)
    return y * (zf * jax.nn.sigmoid(zf))


def _stack_maps(q, dh):
    lane = lax.broadcasted_iota(jnp.int32, q.shape, 1)
    zero = jnp.zeros_like(q)
    return jnp.concatenate([jnp.where(lane < dh, q, zero), jnp.where(lane >= dh, q, zero)], axis=0)


def _flash_kernel(lam_ref, q_ref, k_ref, v_ref, km_ref, vmt_ref, z_ref, g_ref, o_ref, qs_sc, vt_sc, m_sc, acc_sc,
                  *, tq, hd, hb, out_scale):
    i = pl.program_id(2)
    tr = k_ref.shape[0]
    dh = hd // 2
    ones = jnp.ones((ONES_ROWS, tq), bf16)

    @pl.when(i == 0)
    def _():
        for hh in range(hb):
            for c in range(tr // tq):
                vt = v_ref[c * tq:(c + 1) * tq, hh * hd:(hh + 1) * hd].astype(f32).T.astype(bf16)
                vt_sc[hh, :, c * tq:(c + 1) * tq] = jnp.concatenate([vt, ones], axis=0)

    def scores_t(hh, kb):
        return lax.dot_general(kb, qs_sc[hh], NT, preferred_element_type=f32)

    n_meta = km_ref.shape[0]
    for hh in range(hb):
        cols = slice(hh * hd, (hh + 1) * hd)
        qs_sc[hh] = _stack_maps(q_ref[:, cols], dh)
        s = scores_t(hh, km_ref[:, cols])
        m = jnp.max(s, axis=0, keepdims=True)
        m_sc[hh] = m
        vmt = jnp.concatenate([vmt_ref[cols, :], jnp.ones((ONES_ROWS, n_meta), bf16)], axis=0)
        acc_sc[hh] = jnp.dot(vmt, jnp.exp2(s - m).astype(bf16), preferred_element_type=f32)

    def update(off, visible=None):
        hs = range(hb)
        ss = [scores_t(hh, k_ref[pl.ds(off, tq), hh * hd:(hh + 1) * hd]) for hh in hs]
        if visible is not None:
            ss = [jnp.where(visible, s, -jnp.inf) for s in ss]
        m_prev = [m_sc[hh] for hh in hs]
        m_new = [jnp.maximum(m_prev[hh], jnp.max(ss[hh], axis=0, keepdims=True)) for hh in hs]
        ps = [jnp.exp2(ss[hh] - m_new[hh]).astype(bf16) for hh in hs]
        pvs = [jnp.dot(vt_sc[hh, :, pl.ds(off, tq)], ps[hh], preferred_element_type=f32) for hh in hs]
        for hh in hs:
            acc_sc[hh] = jnp.exp2(m_prev[hh] - m_new[hh]) * acc_sc[hh] + pvs[hh]
            m_sc[hh] = m_new[hh]

    def body(j, carry):
        update(pl.multiple_of(j * tq, tq))
        return carry

    lax.fori_loop(0, i, body, 0)
    key = lax.broadcasted_iota(jnp.int32, (tq, 2 * tq), 0)
    qry = lax.broadcasted_iota(jnp.int32, (tq, 2 * tq), 1)
    update(pl.multiple_of(i * tq, tq), key <= jnp.where(qry >= tq, qry - tq, qry))
    for hh in range(hb):
        cols = slice(hh * hd, (hh + 1) * hd)
        acc = acc_sc[hh]
        o = acc[:hd] * (1.0 / acc[hd:hd + 1])
        out = (o[:, :tq] - lam_ref[0] * o[:, tq:]).T
        o_ref[:, cols] = _subln_gate(out, z_ref[:, cols], g_ref[...], out_scale).astype(o_ref.dtype)


def _flash_attention(name, lam, q, k, v, km, vmt, z, g, *, bsz, heads, tq, hb, out_scale):
    mr, w = q.shape
    tr = mr // bsz
    nq = tr // tq
    hd = w // heads
    n_meta = km.shape[0]
    ng = heads // hb
    kern = functools.partial(_flash_kernel, tq=tq, hd=hd, hb=hb, out_scale=out_scale)
    qspec = pl.BlockSpec((tq, hb * hd), lambda b, h, i: (b * nq + i, h))
    return pl.pallas_call(
        kern,
        grid=(bsz, ng, nq),
        in_specs=[pl.BlockSpec(memory_space=pltpu.SMEM),
                  qspec,
                  pl.BlockSpec((tr, hb * hd), lambda b, h, i: (b, h)),
                  pl.BlockSpec((tr, hb * hd), lambda b, h, i: (b, h)),
                  pl.BlockSpec((n_meta, hb * hd), lambda b, h, i: (0, h)),
                  pl.BlockSpec((hb * hd, n_meta), lambda b, h, i: (h, 0)),
                  qspec,
                  pl.BlockSpec((1, hd), lambda b, h, i: (0, 0))],
        out_specs=qspec,
        out_shape=jax.ShapeDtypeStruct((mr, w), bf16),
        scratch_shapes=[pltpu.VMEM((hb, 2 * tq, hd), bf16), pltpu.VMEM((hb, hd + ONES_ROWS, tr), bf16),
                        pltpu.VMEM((hb, 1, 2 * tq), f32), pltpu.VMEM((hb, hd + ONES_ROWS, 2 * tq), f32)],
        compiler_params=_params("arbitrary", "arbitrary", "arbitrary"),
        name=name,
    )(lam, q, k, v, km, vmt, z, g)


def _meta_attn_kernel(lam_ref, q_ref, k_ref, v_ref, z_ref, g_ref, o_ref, *, dh, out_scale):
    r = q_ref.shape[0]
    qs = _stack_maps(q_ref[...].astype(bf16), dh)
    s = lax.dot_general(qs, k_ref[...].astype(bf16), NT, preferred_element_type=f32)
    row = lax.broadcasted_iota(jnp.int32, s.shape, 0)
    row = jnp.where(row >= r, row - r, row)
    col = lax.broadcasted_iota(jnp.int32, s.shape, 1)
    s = jnp.where(col <= row, s, -jnp.inf)
    p = jnp.exp(s - jnp.max(s, axis=-1, keepdims=True))
    l = jnp.sum(p, axis=-1, keepdims=True)
    o = jnp.dot(p.astype(bf16), v_ref[...].astype(bf16), preferred_element_type=f32) * (1.0 / l)
    out = o[:r] - lam_ref[0] * o[r:]
    o_ref[...] = _subln_gate(out, z_ref[...], g_ref[...], out_scale).astype(o_ref.dtype)


def _meta_attention(name, lam, q, k, v, z, g, *, heads, n_meta, row0, out_scale):
    hd = q.shape[1] // heads
    rb = row0 // n_meta
    kern = functools.partial(_meta_attn_kernel, dh=hd // 2, out_scale=out_scale)
    blk = pl.BlockSpec((n_meta, hd), lambda h: (rb, h))
    return pl.pallas_call(
        kern,
        grid=(heads,),
        in_specs=[pl.BlockSpec(memory_space=pltpu.SMEM), blk, blk, blk, blk,
                  pl.BlockSpec((1, hd), lambda h: (0, 0))],
        out_specs=pl.BlockSpec((n_meta, hd), lambda h: (0, h)),
        out_shape=jax.ShapeDtypeStruct((n_meta, q.shape[1]), f32),
        compiler_params=_params("arbitrary"),
        name=name,
    )(lam, q, k, v, z, g)


def _rows_of_head(ref, hh):
    ntok, _, hd = ref.shape
    return ref.reshape(ntok * SUBLANES, hd)[pl.ds(hh, ntok, stride=SUBLANES), :]


def _paged_kernel(pt_ref, lam_ref, qs_ref, *refs, n_steps, pages_per_step, groups, t_new, out_scale):
    del pt_ref
    npg = pages_per_step * groups
    k_refs, v_refs = refs[:npg], refs[npg:2 * npg]
    kn_refs, vn_refs = refs[2 * npg:2 * npg + groups], refs[2 * npg + groups:2 * npg + 2 * groups]
    z_ref, g_ref, o_ref, m_sc, l_sc, acc_sc = refs[2 * npg + 2 * groups:]
    step = pl.program_id(1)
    rows = 2 * t_new
    heads = groups * SUBLANES
    hd = qs_ref.shape[-1]

    @pl.when(step == 0)
    def _():
        m_sc[...] = jnp.full(m_sc.shape, -jnp.inf, f32)
        l_sc[...] = jnp.zeros(l_sc.shape, f32)
        acc_sc[...] = jnp.zeros(acc_sc.shape, f32)

    def head_rows(ref_list, h):
        return _rows_of_head(ref_list[h // SUBLANES], h % SUBLANES).astype(bf16)

    def process(kpages, vpages, causal):
        ntok = kpages[0][0].shape[0]
        s = jnp.concatenate(
            [jnp.concatenate([lax.dot_general(qs_ref[h], head_rows(kg, h), NT, preferred_element_type=f32)
                              for kg in kpages], axis=1) for h in range(heads)], axis=0)
        if causal:
            t = lax.broadcasted_iota(jnp.int32, s.shape, 0) % t_new
            col = lax.broadcasted_iota(jnp.int32, s.shape, 1)
            s = jnp.where(col <= t, s, -jnp.inf)
        m_prev = m_sc[...]
        m_new = jnp.maximum(m_prev, jnp.max(s, axis=-1, keepdims=True))
        alpha = jnp.exp(m_prev - m_new)
        p = jnp.exp(s - m_new)
        l_sc[...] = alpha * l_sc[...] + jnp.sum(p, axis=-1, keepdims=True)
        m_sc[...] = m_new
        pb = p.astype(bf16)
        pv = jnp.concatenate(
            [sum(jnp.dot(pb[h * rows:(h + 1) * rows, j * ntok:(j + 1) * ntok], head_rows(vg, h),
                         preferred_element_type=f32) for j, vg in enumerate(vpages))
             for h in range(heads)], axis=0)
        acc_sc[...] = alpha * acc_sc[...] + pv

    @pl.when(step < n_steps)
    def _():
        process([k_refs[j * groups:(j + 1) * groups] for j in range(pages_per_step)],
                [v_refs[j * groups:(j + 1) * groups] for j in range(pages_per_step)], False)

    @pl.when(step == n_steps)
    def _():
        process([kn_refs], [vn_refs], True)
        o = acc_sc[...] * (1.0 / l_sc[...])
        lam = lam_ref[0]
        for h in range(heads):
            oh = o[h * rows:h * rows + t_new] - lam * o[h * rows + t_new:(h + 1) * rows]
            cols = slice(h * hd, (h + 1) * hd)
            o_ref[:, cols] = _subln_gate(oh, z_ref[:, cols], g_ref[...], out_scale).astype(o_ref.dtype)


def _paged_attention(name, page_table, lam, qs, cache_k, cache_v, kn, vn, z, g, *, layer, t_new, pages_per_step,
                     out_scale):
    dbsz, n_pages = page_table.shape
    pp = pages_per_step
    assert n_pages % pp == 0
    n_steps = n_pages // pp
    page, groups, hd = cache_k.shape[2], cache_k.shape[3], cache_k.shape[5]
    heads = groups * SUBLANES
    w = heads * hd
    rows = 2 * t_new

    def page_spec(j, gidx):
        def imap(b, s, pt):
            return (layer, pt[b, jnp.minimum(s, n_steps - 1) * pp + j], 0, gidx, 0, 0)
        return pl.BlockSpec((None, None, page, None, SUBLANES, hd), imap)

    def new_spec(gidx):
        return pl.BlockSpec((None, kn.shape[1], None, SUBLANES, hd), lambda b, s, pt: (b, 0, gidx, 0, 0))

    page_specs = [page_spec(j, gidx) for j in range(pp) for gidx in range(groups)]
    new_specs = [new_spec(gidx) for gidx in range(groups)]
    kern = functools.partial(_paged_kernel, n_steps=n_steps, pages_per_step=pp, groups=groups, t_new=t_new,
                             out_scale=out_scale)
    grid_spec = pltpu.PrefetchScalarGridSpec(
        num_scalar_prefetch=1,
        grid=(dbsz, n_steps + 1),
        in_specs=([pl.BlockSpec(memory_space=pltpu.SMEM),
                   pl.BlockSpec((None, heads, rows, hd), lambda b, s, pt: (b, 0, 0, 0))]
                  + page_specs + page_specs + new_specs + new_specs
                  + [pl.BlockSpec((t_new, w), lambda b, s, pt: (b, 0)),
                     pl.BlockSpec((1, hd), lambda b, s, pt: (0, 0))]),
        out_specs=pl.BlockSpec((t_new, w), lambda b, s, pt: (b, 0)),
        scratch_shapes=[pltpu.VMEM((heads * rows, 1), f32), pltpu.VMEM((heads * rows, 1), f32),
                        pltpu.VMEM((heads * rows, hd), f32)],
    )
    return pl.pallas_call(
        kern,
        grid_spec=grid_spec,
        out_shape=jax.ShapeDtypeStruct((dbsz * t_new, w), f32),
        compiler_params=_params("arbitrary", "arbitrary"),
        name=name,
    )(page_table, lam, qs, *([cache_k] * (pp * groups)), *([cache_v] * (pp * groups)),
      *([kn] * groups), *([vn] * groups), z, g)


def _mlstm_kernel(m0_ref, uq_ref, uk_ref, pq_ref, pk_ref, wq_ref, wk_ref, bq_ref, bk_ref, v_ref, gi_ref, gf_ref,
                  c0_ref, n0_ref, og_ref, zg_ref, hg_ref, h_ref, c_out, n_out, m_out,
                  padq_sc, padk_sc, c_sc, n_sc, m_sc, *, heads, hb, shared_init, q_scale):
    b, hgrp, c = pl.program_id(0), pl.program_id(1), pl.program_id(2)
    chunk = uq_ref.shape[0]
    taps = wq_ref.shape[0]
    dv, dk = c_sc.shape[1], c_sc.shape[2]
    hs = range(hb)
    base = SUBLANES - (taps - 1)

    @pl.when(c == 0)
    def _():
        c_sc[...] = c0_ref[...]
        n_sc[...] = n0_ref[...]
        for hh in hs:
            m0 = m0_ref[hgrp * hb + hh if shared_init else b * heads + hgrp * hb + hh]
            m_sc[hh] = jnp.full((1, LANES), m0, f32)
        padq_sc[base:SUBLANES, :] = pq_ref[...]
        padk_sc[base:SUBLANES, :] = pk_ref[...]

    def conv_silu(u_ref, w_ref, b_ref, pad_sc):
        pad_sc[SUBLANES:SUBLANES + chunk, :] = u_ref[...]
        out = b_ref[...] + pad_sc[base:base + chunk, :] * w_ref[0:1, :]
        for j in range(1, taps):
            out = out + pad_sc[base + j:base + j + chunk, :] * w_ref[j:j + 1, :]
        pad_sc[base:SUBLANES, :] = pad_sc[base + chunk:SUBLANES + chunk, :]
        return out * jax.nn.sigmoid(out)

    q_all = (conv_silu(uq_ref, wq_ref, bq_ref, padq_sc) * q_scale).astype(bf16)
    k_all = conv_silu(uk_ref, wk_ref, bk_ref, padk_sc).astype(bf16)
    q = [q_all[:, hh * dk:(hh + 1) * dk] for hh in hs]
    k = [k_all[:, hh * dk:(hh + 1) * dk] for hh in hs]
    v = [v_ref[:, hh * dv:(hh + 1) * dv].astype(bf16) for hh in hs]
    c_prev = [c_sc[hh] for hh in hs]
    n_prev = [n_sc[hh] for hh in hs]
    m_prev = [m_sc[hh][:, :1] for hh in hs]
    qk = [lax.dot_general(q[hh], k[hh], NT, preferred_element_type=f32) for hh in hs]
    qc = [lax.dot_general(q[hh], c_prev[hh].astype(bf16), NT, preferred_element_type=f32) for hh in hs]

    li = [gi_ref[hh] for hh in hs]
    lf = [gf_ref[hh] for hh in hs]
    r = lax.broadcasted_iota(jnp.int32, (chunk, chunk), 0)
    s = lax.broadcasted_iota(jnp.int32, (chunk, chunk), 1)
    tri, eye = s <= r, s == r
    lf_col = [jnp.sum(jnp.where(eye, lf[hh], 0.0), axis=1, keepdims=True) for hh in hs]
    li_col = [jnp.sum(jnp.where(eye, li[hh], 0.0), axis=1, keepdims=True) for hh in hs]
    b_col = [jnp.sum(jnp.where(tri, lf[hh], 0.0), axis=1, keepdims=True) for hh in hs]
    b_row = [jnp.sum(jnp.where(r <= s, lf_col[hh], 0.0), axis=0, keepdims=True) for hh in hs]
    a_col = [b_col[hh] + m_prev[hh] for hh in hs]
    d = [jnp.where(tri, b_col[hh] - b_row[hh] + li[hh], -jnp.inf) for hh in hs]
    mt = [jnp.maximum(a_col[hh], jnp.max(d[hh], axis=1, keepdims=True)) for hh in hs]
    w_prev = [jnp.exp(a_col[hh] - mt[hh]) for hh in hs]
    sm = [qk[hh] * jnp.exp(d[hh] - mt[hh]) for hh in hs]
    sv = [jnp.dot(sm[hh].astype(bf16), v[hh], preferred_element_type=f32) for hh in hs]

    b_end = [jnp.sum(lf[hh], axis=1, keepdims=True) for hh in hs]
    a_end = [b_end[hh] + m_prev[hh] for hh in hs]
    m_new = [jnp.maximum(a_end[hh], jnp.max(b_end[hh] - b_row[hh] + li[hh], axis=1, keepdims=True)) for hh in hs]
    w_old = [jnp.exp(a_end[hh] - m_new[hh]) for hh in hs]
    w_tok = [jnp.exp(b_end[hh] - b_col[hh] + li_col[hh] - m_new[hh]) for hh in hs]
    vw = [(v[hh].astype(f32) * w_tok[hh]).astype(bf16) for hh in hs]
    vk = [lax.dot_general(vw[hh], k[hh], TN, preferred_element_type=f32) for hh in hs]
    for hh in hs:
        c_sc[hh] = w_old[hh] * c_prev[hh] + vk[hh]
        n_sc[hh] = w_old[hh] * n_prev[hh] + jnp.sum(k[hh].astype(f32) * w_tok[hh], axis=0, keepdims=True)
        m_sc[hh] = jnp.broadcast_to(m_new[hh], (1, LANES))

    for hh in hs:
        num = sv[hh] + w_prev[hh] * qc[hh]
        qn = jnp.sum(q[hh].astype(f32) * n_prev[hh], axis=1, keepdims=True)
        den = jnp.sum(sm[hh], axis=1, keepdims=True) + w_prev[hh] * qn
        hx = num / jnp.maximum(jnp.abs(den), jnp.exp(-mt[hh]))
        cols = slice(hh * dv, (hh + 1) * dv)
        hn = hx * lax.rsqrt(jnp.mean(hx * hx, axis=1, keepdims=True) + EPS) * hg_ref[:, cols]
        zg = zg_ref[:, cols].astype(f32)
        h_ref[:, cols] = (hn * jax.nn.sigmoid(og_ref[:, cols].astype(f32))
                          * (zg * jax.nn.sigmoid(zg))).astype(h_ref.dtype)

    @pl.when(c == pl.num_programs(2) - 1)
    def _():
        c_out[...] = c_sc[...]
        n_out[...] = n_sc[...]
        m_out[...] = m_sc[...]


def _mlstm(name, mqk, prev, conv_w, conv_b, u5, gi, gf, c0, n0, m0, hg, *, bsz, heads, chunk, row0, out_dtype,
           hb=2):
    nc = gi.shape[2]
    dk = mqk.shape[1] // (2 * heads)
    dv = c0.shape[2]
    taps = conv_w.shape[0]
    rb = row0 // chunk
    shared = c0.shape[0] == 1
    shared_prev = prev.shape[0] == 1
    rows = bsz * nc * chunk

    ng = heads // hb

    def rmap(cb):
        return lambda b, h, c: (rb + b * nc + c, cb(h))

    def smap(b, h, c):
        return (0 if shared else b, h, 0, 0)

    def pmap(cb):
        return lambda b, h, c: (0 if shared_prev else b, 0, cb(h))

    gate = pl.BlockSpec((None, hb, None, 1, chunk), lambda b, h, c: (b, h, c, 0, 0))
    kern = functools.partial(_mlstm_kernel, heads=heads, hb=hb, shared_init=shared, q_scale=dk ** -0.5)
    return pl.pallas_call(
        kern,
        grid=(bsz, ng, nc),
        in_specs=[pl.BlockSpec(memory_space=pltpu.SMEM),
                  pl.BlockSpec((chunk, hb * dk), rmap(lambda h: h)),
                  pl.BlockSpec((chunk, hb * dk), rmap(lambda h: ng + h)),
                  pl.BlockSpec((None, taps - 1, hb * dk), pmap(lambda h: h)),
                  pl.BlockSpec((None, taps - 1, hb * dk), pmap(lambda h: ng + h)),
                  pl.BlockSpec((taps, hb * dk), lambda b, h, c: (0, h)),
                  pl.BlockSpec((taps, hb * dk), lambda b, h, c: (0, ng + h)),
                  pl.BlockSpec((1, hb * dk), lambda b, h, c: (0, h)),
                  pl.BlockSpec((1, hb * dk), lambda b, h, c: (0, ng + h)),
                  pl.BlockSpec((chunk, hb * dv), rmap(lambda h: h)),
                  gate, gate,
                  pl.BlockSpec((None, hb, dv, dk), smap),
                  pl.BlockSpec((None, hb, 1, dk), smap),
                  pl.BlockSpec((chunk, hb * dv), rmap(lambda h: ng + h)),
                  pl.BlockSpec((chunk, hb * dv), rmap(lambda h: 2 * ng + h)),
                  pl.BlockSpec((1, hb * dv), lambda b, h, c: (0, h))],
        out_specs=[pl.BlockSpec((chunk, hb * dv), lambda b, h, c: (b * nc + c, h)),
                   pl.BlockSpec((None, hb, dv, dk), lambda b, h, c: (b, h, 0, 0)),
                   pl.BlockSpec((None, hb, 1, dk), lambda b, h, c: (b, h, 0, 0)),
                   pl.BlockSpec((None, hb, 1, LANES), lambda b, h, c: (b, h, 0, 0))],
        out_shape=[jax.ShapeDtypeStruct((rows, heads * dv), out_dtype),
                   jax.ShapeDtypeStruct((bsz, heads, dv, dk), f32),
                   jax.ShapeDtypeStruct((bsz, heads, 1, dk), f32),
                   jax.ShapeDtypeStruct((bsz, heads, 1, LANES), f32)],
        scratch_shapes=[pltpu.VMEM((chunk + SUBLANES, hb * dk), f32), pltpu.VMEM((chunk + SUBLANES, hb * dk), f32),
                        pltpu.VMEM((hb, dv, dk), f32), pltpu.VMEM((hb, 1, dk), f32), pltpu.VMEM((hb, 1, LANES), f32)],
        compiler_params=_params("arbitrary", "arbitrary", "arbitrary"),
        name=name,
    )(m0, mqk, mqk, prev, prev, conv_w, conv_w, conv_b, conv_b, u5, gi, gf, c0, n0, u5, u5, hg)


def _rms(x, g):
    return x * lax.rsqrt(jnp.mean(x * x, axis=-1, keepdims=True) + EPS) * g


def _gates(g, b_i, b_f, heads, bsz, chunk):
    gi = g[:, :heads] + b_i
    gf = jax.nn.log_sigmoid(g[:, heads:2 * heads] + b_f)

    def lay(a):
        a = a.reshape(bsz, -1, heads).transpose(0, 2, 1)
        return a.reshape(bsz, heads, -1, 1, chunk)
    return lay(gi), lay(gf)


def kernel(x_prompt, x_sample, cache_k, cache_v, page_table, state_conv, state_C, state_n, state_m, meta, norm_g,
           w_in, b_i, b_f, conv_w, conv_b, lam_q1, lam_k1, lam_q2, lam_k2, subln_g, head_g, w_pa, w_pm, w_out,
           norm_f):
    bsz, seq, d = x_prompt.shape
    dbsz, t_new, _ = x_sample.shape
    depth, n_phys, page, h_a, hd_a = cache_k.shape
    dh_a = hd_a // 2
    w_a = h_a * hd_a
    h_m, dv_m, dk_m = state_C.shape[2], state_C.shape[3], state_C.shape[4]
    w_qk, w_m = h_m * dk_m, h_m * dv_m
    n_meta = meta.shape[0]
    taps = conv_w.shape[1]
    n_s = dbsz * t_new
    mr = bsz * seq

    o_q, o_kv, o_z, o_mqk, o_u5 = 0, w_a, 3 * w_a, 4 * w_a, 4 * w_a + 2 * w_qk
    c_gate = o_u5 + 3 * w_m
    w_in = jnp.swapaxes(w_in, 1, 2)
    w_gate = jnp.pad(w_in[:, c_gate:c_gate + 2 * h_m], ((0, 0), (0, LANES - 2 * h_m), (0, 0)))
    w_g = w_in[:, c_gate + 2 * h_m:]
    wpa, wpm, wo = w_pa.astype(bf16), w_pm.astype(bf16), w_out.astype(bf16)
    g_next = jnp.concatenate([norm_g[1:], norm_f[None]], axis=0).reshape(depth, 1, d)
    conv_b2 = conv_b.reshape(depth, 1, 2 * w_qk)

    groups = h_a // SUBLANES
    ck = cache_k.reshape(depth, n_phys, page, groups, SUBLANES, hd_a)
    cv = cache_v.reshape(depth, n_phys, page, groups, SUBLANES, hd_a)

    x_m = x_prompt.reshape(mr, d)
    x_s = jnp.concatenate([x_sample.reshape(n_s, d), meta.astype(f32)], axis=0)
    xn_m = _rms(x_m, norm_g[0]).astype(bf16)
    xn_s = _rms(x_s, norm_g[0]).astype(bf16)
    chunk = 256 if seq % 256 == 0 else seq
    tq = 256 if seq % 256 == 0 else seq
    qscale = dh_a ** -0.5

    outs = {k: [] for k in ("ks", "vs", "cp", "cs", "Cp", "np", "mp", "Cs", "ns", "ms")}
    kp_buf = vp_buf = None
    for l in range(depth):
        lam_init = 0.8 - 0.6 * math.exp(-0.3 * l)
        lam = (jnp.exp(jnp.sum(lam_q1[l] * lam_k1[l])) - jnp.exp(jnp.sum(lam_q2[l] * lam_k2[l]))
               + lam_init).reshape(1).astype(f32)
        out_scale = 1.0 - lam_init
        sg = subln_g[l].reshape(1, hd_a)
        hg = head_g[l].reshape(1, w_m)
        xn_dtype = bf16 if l + 1 < depth else f32

        mm = functools.partial(_matmul, layer=l, tm=2048, tn=512)
        (q_m,), q_s = mm(f"l{l}_q", xn_m, xn_s, w_in, col0=o_q, ncols=w_a, out_dtypes=[bf16],
                         scale=qscale * LOG2E, side_scale=qscale)
        kvp = functools.partial(_kv_proj, depth=depth, bsz=bsz, n_meta=n_meta, meta_row0=n_s, tm=1024, tn=1024)
        k16_m, kp_buf, k_s = kvp(f"l{l}_k", xn_m, xn_s, w_in, l, o_kv, w_a, kp_buf)
        v16_m, vp_buf, v_s = kvp(f"l{l}_v", xn_m, xn_s, w_in, l, o_kv + w_a, w_a, vp_buf)
        (z_m,), z_s = mm(f"l{l}_z", xn_m, xn_s, w_in, col0=o_z, ncols=w_a, out_dtypes=[bf16])
        (mqk_m,), mqk_s = mm(f"l{l}_mqk", xn_m, xn_s, w_in, col0=o_mqk, ncols=2 * w_qk, out_dtypes=[f32])
        (u5_m,), u5_s = mm(f"l{l}_u5", xn_m, xn_s, w_in, col0=o_u5, ncols=3 * w_m, out_dtypes=[bf16])
        (g_m,), g_s = mm(f"l{l}_g", xn_m, xn_s, w_g, col0=0, ncols=2 * d, out_dtypes=[bf16])
        (gt_m,), gt_s = _matmul(f"l{l}_gate", xn_m, xn_s, w_gate, l, 0, LANES, [f32], 2048, LANES)

        mqk_samp = mqk_s[:n_s].reshape(dbsz, t_new, 2 * w_qk)
        outs["cs"].append(jnp.concatenate([state_conv[l], mqk_samp], axis=1)[:, -(taps - 1):])
        gi_samp, gf_samp = _gates(gt_s[:n_s], b_i[l], b_f[l], h_m, dbsz, t_new)
        gi_meta, gf_meta = _gates(gt_s[n_s:], b_i[l], b_f[l], h_m, 1, n_meta)
        hm_samp, c_samp, n_samp, m_samp = _mlstm(
            f"s{l}_mlstm_sample", mqk_s, state_conv[l], conv_w[l], conv_b2[l], u5_s, gi_samp, gf_samp, state_C[l],
            state_n[l].reshape(dbsz, h_m, 1, dk_m), state_m[l].reshape(-1), hg,
            bsz=dbsz, heads=h_m, chunk=t_new, row0=0, out_dtype=f32)
        hm_meta, c_meta, n_meta_s, m_meta = _mlstm(
            f"s{l}_mlstm_meta", mqk_s, jnp.zeros((1, taps - 1, 2 * w_qk), f32), conv_w[l], conv_b2[l], u5_s,
            gi_meta, gf_meta, jnp.zeros((1, h_m, dv_m, dk_m), f32), jnp.zeros((1, h_m, 1, dk_m), f32),
            jnp.zeros((h_m,), f32), hg, bsz=1, heads=h_m, chunk=n_meta, row0=n_s, out_dtype=f32)
        outs["Cs"].append(c_samp)
        outs["ns"].append(n_samp.reshape(dbsz, h_m, dk_m))
        outs["ms"].append(m_samp[:, :, 0, 0])

        qs = q_s[:n_s].reshape(dbsz, t_new, h_a, hd_a).transpose(0, 2, 1, 3)
        lane = jnp.arange(hd_a)
        qs = jnp.concatenate([jnp.where(lane < dh_a, qs, 0.0), jnp.where(lane >= dh_a, qs, 0.0)], axis=2).astype(bf16)

        def new_page(a):
            a = a.reshape(dbsz, t_new, groups, SUBLANES, hd_a)
            return jnp.pad(a, ((0, 0), (0, page - t_new), (0, 0), (0, 0), (0, 0)))
        att_samp = _paged_attention(f"s{l}_paged_attn", page_table, lam, qs, ck, cv, new_page(k_s[:n_s]),
                                    new_page(v_s[:n_s]), z_s, sg, layer=l, t_new=t_new,
                                    pages_per_step=8 if page_table.shape[1] % 8 == 0 else 4,
                                    out_scale=out_scale)
        att_meta = _meta_attention(f"s{l}_meta_attn", lam, q_s, k_s, v_s, z_s, sg, heads=h_a, n_meta=n_meta,
                                   row0=n_s, out_scale=out_scale)
        att_s = jnp.concatenate([att_samp, att_meta], axis=0)
        hm_s = jnp.concatenate([hm_samp, hm_meta], axis=0)
        x_s, xn_s = _out_proj(f"s{l}_out", att_s, hm_s, g_s, x_s, wpa, wpm, wo, g_next[l], l, 256, xn_dtype)
        outs["ks"].append(k_s[:n_s].reshape(dbsz, t_new, h_a, hd_a))
        outs["vs"].append(v_s[:n_s].reshape(dbsz, t_new, h_a, hd_a))

        outs["cp"].append(mqk_m.reshape(bsz, seq, 2 * w_qk)[:, -(taps - 1):])
        gi_m, gf_m = _gates(gt_m, b_i[l], b_f[l], h_m, bsz, chunk)
        prev_m = mqk_s[n_s + n_meta - (taps - 1):].reshape(1, taps - 1, 2 * w_qk)
        hm_m, c_p, n_p, m_p = _mlstm(
            f"m{l}_mlstm", mqk_m, prev_m, conv_w[l], conv_b2[l], u5_m, gi_m, gf_m, c_meta, n_meta_s,
            m_meta[:, :, 0, 0].reshape(-1), hg, bsz=bsz, heads=h_m, chunk=chunk, row0=0, out_dtype=bf16)
        outs["Cp"].append(c_p)
        outs["np"].append(n_p.reshape(bsz, h_m, dk_m))
        outs["mp"].append(m_p[:, :, 0, 0])

        km = k_s[n_s:].astype(bf16)
        vmt = v_s[n_s:].T.astype(bf16)
        att_m = _flash_attention(f"m{l}_flash_attn", lam, q_m, k16_m, v16_m, km, vmt, z_m, sg, bsz=bsz, heads=h_a,
                                 tq=tq, hb=4, out_scale=out_scale)
        x_m, xn_m = _out_proj(f"m{l}_out", att_m, hm_m, g_m, x_m, wpa, wpm, wo, g_next[l], l, 256, xn_dtype)

    y_prompt = xn_m.reshape(bsz, seq, d)
    y_sample = xn_s[:n_s].reshape(dbsz, t_new, d)
    st = {k: jnp.stack(v) for k, v in outs.items()}
    kp = kp_buf.reshape(depth, bsz, n_meta + seq, h_a, hd_a)
    vp = vp_buf.reshape(depth, bsz, n_meta + seq, h_a, hd_a)
    return (y_prompt, y_sample, kp, vp, st["ks"], st["vs"], st["cp"], st["cs"],
            st["Cp"], st["np"], st["mp"], st["Cs"], st["ns"], st["ms"])
```

```python
import functools
import math

import jax
import jax.numpy as jnp
from jax import lax
from jax.experimental import pallas as pl
from jax.experimental.pallas import tpu as pltpu

f32 = jnp.float32
bf16 = jnp.bfloat16

EPS = 1e-6
LANES = 128
SUBLANES = 8
ONES_ROWS = 16
LOG2E = math.log2(math.e)
V7X_VMEM_LIMIT = 56 * 1024 * 1024
NT = (((1,), (1,)), ((), ()))
TN = (((0,), (0,)), ((), ()))


def _params(*sem):
    return pltpu.CompilerParams(dimension_semantics=sem, vmem_limit_bytes=V7X_VMEM_LIMIT)


def _mm_kernel(x_ref, xs_ref, w_ref, *refs, scale, side_scale):
    o_refs, os_ref, wb_sc = refs[:-2], refs[-2], refs[-1]

    @pl.when(pl.program_id(1) == 0)
    def _():
        wb_sc[...] = w_ref[...].astype(bf16)
        os_ref[...] = lax.dot_general(xs_ref[...], wb_sc[...], NT, preferred_element_type=f32) * side_scale

    acc = lax.dot_general(x_ref[...], wb_sc[...], NT, preferred_element_type=f32)
    if scale != 1.0:
        acc = acc * scale
    for o_ref in o_refs:
        o_ref[...] = acc.astype(o_ref.dtype)


def _matmul(name, x, xs, w, layer, col0, ncols, out_dtypes, tm, tn, scale=1.0, side_scale=1.0):
    m, k = x.shape
    ms = xs.shape[0]
    tm = min(tm, m)
    tn = min(tn, ncols)
    assert col0 % tn == 0 and ncols % tn == 0 and m % tm == 0
    res = pl.pallas_call(
        functools.partial(_mm_kernel, scale=scale, side_scale=side_scale),
        grid=(ncols // tn, m // tm),
        in_specs=[pl.BlockSpec((tm, k), lambda j, i: (i, 0)),
                  pl.BlockSpec((ms, k), lambda j, i: (0, 0)),
                  pl.BlockSpec((None, tn, k), lambda j, i: (layer, col0 // tn + j, 0))],
        out_specs=([pl.BlockSpec((tm, tn), lambda j, i: (i, j)) for _ in out_dtypes]
                   + [pl.BlockSpec((ms, tn), lambda j, i: (0, j))]),
        out_shape=([jax.ShapeDtypeStruct((m, ncols), dt) for dt in out_dtypes]
                   + [jax.ShapeDtypeStruct((ms, ncols), f32)]),
        scratch_shapes=[pltpu.VMEM((tn, k), bf16)],
        compiler_params=_params("arbitrary", "arbitrary"),
        name=name,
    )(x, xs, w)
    return res[:-1], res[-1]


def _store_rows_of_head(ref, hh, row0, val):
    ntok, _, hd = ref.shape
    ref.reshape(ntok * SUBLANES, hd)[pl.ds(row0 * SUBLANES + hh, val.shape[0], stride=SUBLANES), :] = val


def _kv_kernel(x_ref, xs_ref, w_ref, *refs, nt, shift, meta_row0, aliased):
    if aliased:
        refs = refs[1:]
    o16_ref, out_ref, os_ref, wb_sc, meta_sc, carry_sc = refs
    b, i = pl.program_id(1), pl.program_id(2)
    tm = x_ref.shape[0]

    @pl.when((b == 0) & (i == 0))
    def _():
        wb_sc[...] = w_ref[...].astype(bf16)
        side = lax.dot_general(xs_ref[...], wb_sc[...], NT, preferred_element_type=f32)
        os_ref[...] = side
        meta_sc[...] = side[meta_row0:meta_row0 + shift]

    @pl.when(i == 0)
    def _():
        carry_sc[...] = meta_sc[...]

    hd = out_ref.shape[2]

    def put(row0, val):
        for hh in range(SUBLANES):
            _store_rows_of_head(out_ref, hh, row0, val[:, hh * hd:(hh + 1) * hd])

    @pl.when(i < nt)
    def _():
        res = lax.dot_general(x_ref[...], wb_sc[...], NT, preferred_element_type=f32)
        o16_ref[...] = res.astype(o16_ref.dtype)
        put(0, carry_sc[...])
        put(shift, res[:tm - shift])
        carry_sc[...] = res[tm - shift:]

    @pl.when(i == nt)
    def _():
        out_ref[shift:] = jnp.zeros((tm - shift,) + out_ref.shape[1:], f32)
        put(0, carry_sc[...])


def _kv_proj(name, x, xs, w, layer, col0, ncols, buf, *, depth, bsz, n_meta, meta_row0, tm, tn):
    m, k = x.shape
    ms = xs.shape[0]
    seq = m // bsz
    tm = min(tm, seq)
    nt = seq // tm
    assert seq % tm == 0 and col0 % tn == 0 and ncols % tn == 0 and n_meta % SUBLANES == 0 and n_meta < tm
    hd = tn // SUBLANES
    assert hd % LANES == 0
    aliased = buf is not None
    kern = functools.partial(_kv_kernel, nt=nt, shift=n_meta, meta_row0=meta_row0, aliased=aliased)

    def xmap(j, b, i):
        return (b * nt + jnp.minimum(i, nt - 1), 0)

    in_specs = [pl.BlockSpec((tm, k), xmap),
                pl.BlockSpec((ms, k), lambda j, b, i: (0, 0)),
                pl.BlockSpec((None, tn, k), lambda j, b, i: (layer, col0 // tn + j, 0))]
    args = [x, xs, w]
    if aliased:
        in_specs.append(pl.BlockSpec(memory_space=pl.ANY))
        args.append(buf)
    o16, out, side = pl.pallas_call(
        kern,
        grid=(ncols // tn, bsz, nt + 1),
        in_specs=in_specs,
        out_specs=[pl.BlockSpec((tm, tn), lambda j, b, i: (b * nt + jnp.minimum(i, nt - 1), j)),
                   pl.BlockSpec((None, None, tm, None, SUBLANES, hd), lambda j, b, i: (layer, b, i, j, 0, 0)),
                   pl.BlockSpec((ms, tn), lambda j, b, i: (0, j))],
        out_shape=[jax.ShapeDtypeStruct((m, ncols), bf16),
                   jax.ShapeDtypeStruct((depth, bsz, n_meta + seq, ncols // tn, SUBLANES, hd), f32),
                   jax.ShapeDtypeStruct((ms, ncols), f32)],
        scratch_shapes=[pltpu.VMEM((tn, k), bf16), pltpu.VMEM((n_meta, tn), f32), pltpu.VMEM((n_meta, tn), f32)],
        input_output_aliases={3: 1} if aliased else {},
        compiler_params=_params("arbitrary", "arbitrary", "arbitrary"),
        name=name,
    )(*args)
    return o16, out, side


def _out_kernel(a_ref, m_ref, ga_ref, gm_ref, x_ref, wpa_ref, wpm_ref, wo_ref, gn_ref, o_ref, xn_ref):
    ya = jnp.dot(a_ref[...].astype(bf16), wpa_ref[...], preferred_element_type=f32)
    ym = jnp.dot(m_ref[...].astype(bf16), wpm_ref[...], preferred_element_type=f32)
    merged = (jax.nn.sigmoid(ga_ref[...].astype(f32)) * ya
              + jax.nn.sigmoid(gm_ref[...].astype(f32)) * ym)
    x_new = x_ref[...] + jnp.dot(merged.astype(bf16), wo_ref[...], preferred_element_type=f32)
    o_ref[...] = x_new
    xn = x_new * lax.rsqrt(jnp.mean(x_new * x_new, axis=-1, keepdims=True) + EPS) * gn_ref[...]
    xn_ref[...] = xn.astype(xn_ref.dtype)


def _out_proj(name, a, mm, g, x, w_pa, w_pm, w_out, g_next, layer, tm, xn_dtype):
    m, d = x.shape
    tm = min(tm, m)
    wspec = pl.BlockSpec((None, d, d), lambda i: (layer, 0, 0), pipeline_mode=pl.Buffered(1))
    row = pl.BlockSpec((tm, d), lambda i: (i, 0))
    return pl.pallas_call(
        _out_kernel,
        grid=(pl.cdiv(m, tm),),
        in_specs=[row, row, row, pl.BlockSpec((tm, d), lambda i: (i, 1)), row, wspec, wspec, wspec,
                  pl.BlockSpec((1, d), lambda i: (0, 0))],
        out_specs=[row, row],
        out_shape=[jax.ShapeDtypeStruct((m, d), f32), jax.ShapeDtypeStruct((m, d), xn_dtype)],
        compiler_params=_params("arbitrary"),
        name=name,
    )(a, mm, g, g, x, w_pa, w_pm, w_out, g_next)


def _subln_gate(o, z, g, scale):
    y = o * lax.rsqrt(jnp.mean(o * o, axis=-1, keepdims=True) + EPS) * (g * scale)
    zf = z.astype(f32)
    return y * (zf * jax.nn.sigmoid(zf))


def _stack_maps(q, dh):
    lane = lax.broadcasted_iota(jnp.int32, q.shape, 1)
    zero = jnp.zeros_like(q)
    return jnp.concatenate([jnp.where(lane < dh, q, zero), jnp.where(lane >= dh, q, zero)], axis=0)


def _flash_kernel(lam_ref, q_ref, k_ref, v_ref, km_ref, vmt_ref, z_ref, g_ref, o_ref, qs_sc, vt_sc, m_sc, acc_sc,
                  *, tq, hd, hb, out_scale):
    i = pl.program_id(2)
    tr = k_ref.shape[0]
    dh = hd // 2
    ones = jnp.ones((ONES_ROWS, tq), bf16)

    @pl.when(i == 0)
    def _():
        for hh in range(hb):
            for c in range(tr // tq):
                vt = v_ref[c * tq:(c + 1) * tq, hh * hd:(hh + 1) * hd].astype(f32).T.astype(bf16)
                vt_sc[hh, :, c * tq:(c + 1) * tq] = jnp.concatenate([vt, ones], axis=0)

    def scores_t(hh, kb):
        return lax.dot_general(kb, qs_sc[hh], NT, preferred_element_type=f32)

    hs = range(hb)
    n_meta = km_ref.shape[0]
    head_cols = [slice(hh * hd, (hh + 1) * hd) for hh in hs]
    for hh in hs:
        qs_sc[hh] = _stack_maps(q_ref[:, head_cols[hh]], dh)

    off_d = pl.multiple_of(i * tq, tq)
    key = lax.broadcasted_iota(jnp.int32, (n_meta + tq, 2 * tq), 0)
    qry = lax.broadcasted_iota(jnp.int32, (n_meta + tq, 2 * tq), 1)
    visible = key - n_meta <= jnp.where(qry >= tq, qry - tq, qry)
    ss = [jnp.where(visible, scores_t(hh, jnp.concatenate([km_ref[:, head_cols[hh]],
                                                           k_ref[pl.ds(off_d, tq), head_cols[hh]]], axis=0)),
                    -jnp.inf) for hh in hs]
    ms = [jnp.max(ss[hh], axis=0, keepdims=True) for hh in hs]
    ps = [jnp.exp2(ss[hh] - ms[hh]).astype(bf16) for hh in hs]
    ones_m = jnp.ones((ONES_ROWS, n_meta), bf16)
    for hh in hs:
        vmt = jnp.concatenate([vmt_ref[head_cols[hh], :], ones_m], axis=0)
        acc_sc[hh] = (jnp.dot(vmt, ps[hh][:n_meta], preferred_element_type=f32)
                      + jnp.dot(vt_sc[hh, :, pl.ds(off_d, tq)], ps[hh][n_meta:], preferred_element_type=f32))
        m_sc[hh] = ms[hh]

    def update(off, nkeys):
        ss = [scores_t(hh, k_ref[pl.ds(off, nkeys), head_cols[hh]]) for hh in hs]
        m_prev = [m_sc[hh] for hh in hs]
        m_new = [jnp.maximum(m_prev[hh], jnp.max(ss[hh], axis=0, keepdims=True)) for hh in hs]
        ps = [jnp.exp2(ss[hh] - m_new[hh]).astype(bf16) for hh in hs]
        pvs = [jnp.dot(vt_sc[hh, :, pl.ds(off, nkeys)], ps[hh], preferred_element_type=f32) for hh in hs]
        for hh in hs:
            acc_sc[hh] = jnp.exp2(m_prev[hh] - m_new[hh]) * acc_sc[hh] + pvs[hh]
            m_sc[hh] = m_new[hh]

    def body(j, carry):
        update(pl.multiple_of(j * (2 * tq), 2 * tq), 2 * tq)
        return carry

    lax.fori_loop(0, i // 2, body, 0)

    @pl.when(i % 2 == 1)
    def _():
        update(pl.multiple_of((i - 1) * tq, tq), tq)

    for hh in range(hb):
        cols = slice(hh * hd, (hh + 1) * hd)
        acc = acc_sc[hh]
        o = acc[:hd] * (1.0 / acc[hd:hd + 1])
        out = (o[:, :tq] - lam_ref[0] * o[:, tq:]).T
        o_ref[:, cols] = _subln_gate(out, z_ref[:, cols], g_ref[...], out_scale).astype(o_ref.dtype)


def _flash_attention(name, lam, q, k, v, km, vmt, z, g, *, bsz, heads, tq, hb, out_scale):
    mr, w = q.shape
    tr = mr // bsz
    nq = tr // tq
    hd = w // heads
    n_meta = km.shape[0]
    ng = heads // hb
    kern = functools.partial(_flash_kernel, tq=tq, hd=hd, hb=hb, out_scale=out_scale)
    qspec = pl.BlockSpec((tq, hb * hd), lambda b, h, i: (b * nq + i, h))
    return pl.pallas_call(
        kern,
        grid=(bsz, ng, nq),
        in_specs=[pl.BlockSpec(memory_space=pltpu.SMEM),
                  qspec,
                  pl.BlockSpec((tr, hb * hd), lambda b, h, i: (b, h)),
                  pl.BlockSpec((tr, hb * hd), lambda b, h, i: (b, h)),
                  pl.BlockSpec((n_meta, hb * hd), lambda b, h, i: (0, h)),
                  pl.BlockSpec((hb * hd, n_meta), lambda b, h, i: (h, 0)),
                  qspec,
                  pl.BlockSpec((1, hd), lambda b, h, i: (0, 0))],
        out_specs=qspec,
        out_shape=jax.ShapeDtypeStruct((mr, w), bf16),
        scratch_shapes=[pltpu.VMEM((hb, 2 * tq, hd), bf16), pltpu.VMEM((hb, hd + ONES_ROWS, tr), bf16),
                        pltpu.VMEM((hb, 1, 2 * tq), f32), pltpu.VMEM((hb, hd + ONES_ROWS, 2 * tq), f32)],
        compiler_params=_params("arbitrary", "arbitrary", "arbitrary"),
        name=name,
    )(lam, q, k, v, km, vmt, z, g)


def _meta_attn_kernel(lam_ref, q_ref, k_ref, v_ref, z_ref, g_ref, o_ref, *, dh, out_scale):
    r = q_ref.shape[0]
    qs = _stack_maps(q_ref[...].astype(bf16), dh)
    s = lax.dot_general(qs, k_ref[...].astype(bf16), NT, preferred_element_type=f32)
    row = lax.broadcasted_iota(jnp.int32, s.shape, 0)
    row = jnp.where(row >= r, row - r, row)
    col = lax.broadcasted_iota(jnp.int32, s.shape, 1)
    s = jnp.where(col <= row, s, -jnp.inf)
    p = jnp.exp(s - jnp.max(s, axis=-1, keepdims=True))
    l = jnp.sum(p, axis=-1, keepdims=True)
    o = jnp.dot(p.astype(bf16), v_ref[...].astype(bf16), preferred_element_type=f32) * (1.0 / l)
    out = o[:r] - lam_ref[0] * o[r:]
    o_ref[...] = _subln_gate(out, z_ref[...], g_ref[...], out_scale).astype(o_ref.dtype)


def _meta_attention(name, lam, q, k, v, z, g, *, heads, n_meta, row0, out_scale):
    hd = q.shape[1] // heads
    rb = row0 // n_meta
    kern = functools.partial(_meta_attn_kernel, dh=hd // 2, out_scale=out_scale)
    blk = pl.BlockSpec((n_meta, hd), lambda h: (rb, h))
    return pl.pallas_call(
        kern,
        grid=(heads,),
        in_specs=[pl.BlockSpec(memory_space=pltpu.SMEM), blk, blk, blk, blk,
                  pl.BlockSpec((1, hd), lambda h: (0, 0))],
        out_specs=pl.BlockSpec((n_meta, hd), lambda h: (0, h)),
        out_shape=jax.ShapeDtypeStruct((n_meta, q.shape[1]), f32),
        compiler_params=_params("arbitrary"),
        name=name,
    )(lam, q, k, v, z, g)


def _rows_of_head(ref, hh):
    ntok, _, hd = ref.shape
    return ref.reshape(ntok * SUBLANES, hd)[pl.ds(hh, ntok, stride=SUBLANES), :]


def _paged_kernel(pt_ref, lam_ref, qs_ref, *refs, n_steps, pages_per_step, groups, t_new, out_scale):
    del pt_ref
    npg = pages_per_step * groups
    k_refs, v_refs = refs[:npg], refs[npg:2 * npg]
    kn_refs, vn_refs = refs[2 * npg:2 * npg + groups], refs[2 * npg + groups:2 * npg + 2 * groups]
    z_ref, g_ref, o_ref, m_sc, l_sc, acc_sc = refs[2 * npg + 2 * groups:]
    step = pl.program_id(1)
    rows = 2 * t_new
    heads = groups * SUBLANES
    hd = qs_ref.shape[-1]

    @pl.when(step == 0)
    def _():
        m_sc[...] = jnp.full(m_sc.shape, -jnp.inf, f32)
        l_sc[...] = jnp.zeros(l_sc.shape, f32)
        acc_sc[...] = jnp.zeros(acc_sc.shape, f32)

    def head_rows(ref_list, h):
        return _rows_of_head(ref_list[h // SUBLANES], h % SUBLANES).astype(bf16)

    def process(kpages, vpages, causal):
        ntok = kpages[0][0].shape[0]
        s = jnp.concatenate(
            [jnp.concatenate([lax.dot_general(qs_ref[h], head_rows(kg, h), NT, preferred_element_type=f32)
                              for kg in kpages], axis=1) for h in range(heads)], axis=0)
        if causal:
            t = lax.broadcasted_iota(jnp.int32, s.shape, 0) % t_new
            col = lax.broadcasted_iota(jnp.int32, s.shape, 1)
            s = jnp.where(col <= t, s, -jnp.inf)
        m_prev = m_sc[...]
        m_new = jnp.maximum(m_prev, jnp.max(s, axis=-1, keepdims=True))
        alpha = jnp.exp(m_prev - m_new)
        p = jnp.exp(s - m_new)
        l_sc[...] = alpha * l_sc[...] + jnp.sum(p, axis=-1, keepdims=True)
        m_sc[...] = m_new
        pb = p.astype(bf16)
        pv = jnp.concatenate(
            [sum(jnp.dot(pb[h * rows:(h + 1) * rows, j * ntok:(j + 1) * ntok], head_rows(vg, h),
                         preferred_element_type=f32) for j, vg in enumerate(vpages))
             for h in range(heads)], axis=0)
        acc_sc[...] = alpha * acc_sc[...] + pv

    @pl.when(step < n_steps)
    def _():
        process([k_refs[j * groups:(j + 1) * groups] for j in range(pages_per_step)],
                [v_refs[j * groups:(j + 1) * groups] for j in range(pages_per_step)], False)

    @pl.when(step == n_steps)
    def _():
        process([kn_refs], [vn_refs], True)
        o = acc_sc[...] * (1.0 / l_sc[...])
        lam = lam_ref[0]
        for h in range(heads):
            oh = o[h * rows:h * rows + t_new] - lam * o[h * rows + t_new:(h + 1) * rows]
            cols = slice(h * hd, (h + 1) * hd)
            o_ref[:, cols] = _subln_gate(oh, z_ref[:, cols], g_ref[...], out_scale).astype(o_ref.dtype)


def _paged_attention(name, page_table, lam, qs, cache_k, cache_v, kn, vn, z, g, *, layer, t_new, pages_per_step,
                     out_scale):
    dbsz, n_pages = page_table.shape
    pp = pages_per_step
    assert n_pages % pp == 0
    n_steps = n_pages // pp
    page, groups, hd = cache_k.shape[2], cache_k.shape[3], cache_k.shape[5]
    heads = groups * SUBLANES
    w = heads * hd
    rows = 2 * t_new

    def page_spec(j, gidx):
        def imap(b, s, pt):
            return (layer, pt[b, jnp.minimum(s, n_steps - 1) * pp + j], 0, gidx, 0, 0)
        return pl.BlockSpec((None, None, page, None, SUBLANES, hd), imap)

    def new_spec(gidx):
        return pl.BlockSpec((None, kn.shape[1], None, SUBLANES, hd), lambda b, s, pt: (b, 0, gidx, 0, 0))

    page_specs = [page_spec(j, gidx) for j in range(pp) for gidx in range(groups)]
    new_specs = [new_spec(gidx) for gidx in range(groups)]
    kern = functools.partial(_paged_kernel, n_steps=n_steps, pages_per_step=pp, groups=groups, t_new=t_new,
                             out_scale=out_scale)
    grid_spec = pltpu.PrefetchScalarGridSpec(
        num_scalar_prefetch=1,
        grid=(dbsz, n_steps + 1),
        in_specs=([pl.BlockSpec(memory_space=pltpu.SMEM),
                   pl.BlockSpec((None, heads, rows, hd), lambda b, s, pt: (b, 0, 0, 0))]
                  + page_specs + page_specs + new_specs + new_specs
                  + [pl.BlockSpec((t_new, w), lambda b, s, pt: (b, 0)),
                     pl.BlockSpec((1, hd), lambda b, s, pt: (0, 0))]),
        out_specs=pl.BlockSpec((t_new, w), lambda b, s, pt: (b, 0)),
        scratch_shapes=[pltpu.VMEM((heads * rows, 1), f32), pltpu.VMEM((heads * rows, 1), f32),
                        pltpu.VMEM((heads * rows, hd), f32)],
    )
    return pl.pallas_call(
        kern,
        grid_spec=grid_spec,
        out_shape=jax.ShapeDtypeStruct((dbsz * t_new, w), f32),
        compiler_params=_params("arbitrary", "arbitrary"),
        name=name,
    )(page_table, lam, qs, *([cache_k] * (pp * groups)), *([cache_v] * (pp * groups)),
      *([kn] * groups), *([vn] * groups), z, g)


def _mlstm_kernel(m0_ref, uq_ref, uk_ref, pq_ref, pk_ref, wq_ref, wk_ref, bq_ref, bk_ref, v_ref, gi_ref, gf_ref,
                  c0_ref, n0_ref, og_ref, zg_ref, hg_ref, h_ref, c_out, n_out, m_out,
                  padq_sc, padk_sc, c_sc, n_sc, m_sc, *, heads, hb, shared_init, q_scale):
    b, hgrp, c = pl.program_id(0), pl.program_id(1), pl.program_id(2)
    chunk = uq_ref.shape[0]
    taps = wq_ref.shape[0]
    dv, dk = c_sc.shape[1], c_sc.shape[2]
    hs = range(hb)
    base = SUBLANES - (taps - 1)

    @pl.when(c == 0)
    def _():
        c_sc[...] = c0_ref[...]
        n_sc[...] = n0_ref[...]
        for hh in hs:
            m0 = m0_ref[hgrp * hb + hh if shared_init else b * heads + hgrp * hb + hh]
            m_sc[hh] = jnp.full((1, LANES), m0, f32)
        padq_sc[base:SUBLANES, :] = pq_ref[...]
        padk_sc[base:SUBLANES, :] = pk_ref[...]

    def conv_silu(u_ref, w_ref, b_ref, pad_sc):
        pad_sc[SUBLANES:SUBLANES + chunk, :] = u_ref[...]
        out = b_ref[...] + pad_sc[base:base + chunk, :] * w_ref[0:1, :]
        for j in range(1, taps):
            out = out + pad_sc[base + j:base + j + chunk, :] * w_ref[j:j + 1, :]
        pad_sc[base:SUBLANES, :] = pad_sc[base + chunk:SUBLANES + chunk, :]
        return out * jax.nn.sigmoid(out)

    q_all = (conv_silu(uq_ref, wq_ref, bq_ref, padq_sc) * q_scale).astype(bf16)
    k_all = conv_silu(uk_ref, wk_ref, bk_ref, padk_sc).astype(bf16)
    q = [q_all[:, hh * dk:(hh + 1) * dk] for hh in hs]
    k = [k_all[:, hh * dk:(hh + 1) * dk] for hh in hs]
    v = [v_ref[:, hh * dv:(hh + 1) * dv].astype(bf16) for hh in hs]
    c_prev = [c_sc[hh] for hh in hs]
    n_prev = [n_sc[hh] for hh in hs]
    m_prev = [m_sc[hh][:, :1] for hh in hs]
    qk = [lax.dot_general(q[hh], k[hh], NT, preferred_element_type=f32) for hh in hs]
    qc = [lax.dot_general(q[hh], c_prev[hh].astype(bf16), NT, preferred_element_type=f32) for hh in hs]

    li = [gi_ref[hh] for hh in hs]
    lf = [gf_ref[hh] for hh in hs]
    r = lax.broadcasted_iota(jnp.int32, (chunk, chunk), 0)
    s = lax.broadcasted_iota(jnp.int32, (chunk, chunk), 1)
    tri, eye = s <= r, s == r
    lf_col = [jnp.sum(jnp.where(eye, lf[hh], 0.0), axis=1, keepdims=True) for hh in hs]
    li_col = [jnp.sum(jnp.where(eye, li[hh], 0.0), axis=1, keepdims=True) for hh in hs]
    b_col = [jnp.sum(jnp.where(tri, lf[hh], 0.0), axis=1, keepdims=True) for hh in hs]
    b_row = [jnp.sum(jnp.where(r <= s, lf_col[hh], 0.0), axis=0, keepdims=True) for hh in hs]
    a_col = [b_col[hh] + m_prev[hh] for hh in hs]
    d = [jnp.where(tri, b_col[hh] - b_row[hh] + li[hh], -jnp.inf) for hh in hs]
    mt = [jnp.maximum(a_col[hh], jnp.max(d[hh], axis=1, keepdims=True)) for hh in hs]
    w_prev = [jnp.exp(a_col[hh] - mt[hh]) for hh in hs]
    sm = [qk[hh] * jnp.exp(d[hh] - mt[hh]) for hh in hs]
    sv = [jnp.dot(sm[hh].astype(bf16), v[hh], preferred_element_type=f32) for hh in hs]

    b_end = [jnp.sum(lf[hh], axis=1, keepdims=True) for hh in hs]
    a_end = [b_end[hh] + m_prev[hh] for hh in hs]
    m_new = [jnp.maximum(a_end[hh], jnp.max(b_end[hh] - b_row[hh] + li[hh], axis=1, keepdims=True)) for hh in hs]
    w_old = [jnp.exp(a_end[hh] - m_new[hh]) for hh in hs]
    w_tok = [jnp.exp(b_end[hh] - b_col[hh] + li_col[hh] - m_new[hh]) for hh in hs]
    vw = [(v[hh].astype(f32) * w_tok[hh]).astype(bf16) for hh in hs]
    vk = [lax.dot_general(vw[hh], k[hh], TN, preferred_element_type=f32) for hh in hs]
    for hh in hs:
        c_sc[hh] = w_old[hh] * c_prev[hh] + vk[hh]
        n_sc[hh] = w_old[hh] * n_prev[hh] + jnp.sum(k[hh].astype(f32) * w_tok[hh], axis=0, keepdims=True)
        m_sc[hh] = jnp.broadcast_to(m_new[hh], (1, LANES))

    for hh in hs:
        num = sv[hh] + w_prev[hh] * qc[hh]
        qn = jnp.sum(q[hh].astype(f32) * n_prev[hh], axis=1, keepdims=True)
        den = jnp.sum(sm[hh], axis=1, keepdims=True) + w_prev[hh] * qn
        hx = num / jnp.maximum(jnp.abs(den), jnp.exp(-mt[hh]))
        cols = slice(hh * dv, (hh + 1) * dv)
        hn = hx * lax.rsqrt(jnp.mean(hx * hx, axis=1, keepdims=True) + EPS) * hg_ref[:, cols]
        zg = zg_ref[:, cols].astype(f32)
        h_ref[:, cols] = (hn * jax.nn.sigmoid(og_ref[:, cols].astype(f32))
                          * (zg * jax.nn.sigmoid(zg))).astype(h_ref.dtype)

    @pl.when(c == pl.num_programs(2) - 1)
    def _():
        c_out[...] = c_sc[...]
        n_out[...] = n_sc[...]
        m_out[...] = m_sc[...]


def _mlstm(name, mqk, prev, conv_w, conv_b, u5, gi, gf, c0, n0, m0, hg, *, bsz, heads, chunk, row0, out_dtype,
           hb=2):
    nc = gi.shape[2]
    dk = mqk.shape[1] // (2 * heads)
    dv = c0.shape[2]
    taps = conv_w.shape[0]
    rb = row0 // chunk
    shared = c0.shape[0] == 1
    shared_prev = prev.shape[0] == 1
    rows = bsz * nc * chunk

    ng = heads // hb

    def rmap(cb):
        return lambda b, h, c: (rb + b * nc + c, cb(h))

    def smap(b, h, c):
        return (0 if shared else b, h, 0, 0)

    def pmap(cb):
        return lambda b, h, c: (0 if shared_prev else b, 0, cb(h))

    gate = pl.BlockSpec((None, hb, None, 1, chunk), lambda b, h, c: (b, h, c, 0, 0))
    kern = functools.partial(_mlstm_kernel, heads=heads, hb=hb, shared_init=shared, q_scale=dk ** -0.5)
    return pl.pallas_call(
        kern,
        grid=(bsz, ng, nc),
        in_specs=[pl.BlockSpec(memory_space=pltpu.SMEM),
                  pl.BlockSpec((chunk, hb * dk), rmap(lambda h: h)),
                  pl.BlockSpec((chunk, hb * dk), rmap(lambda h: ng + h)),
                  pl.BlockSpec((None, taps - 1, hb * dk), pmap(lambda h: h)),
                  pl.BlockSpec((None, taps - 1, hb * dk), pmap(lambda h: ng + h)),
                  pl.BlockSpec((taps, hb * dk), lambda b, h, c: (0, h)),
                  pl.BlockSpec((taps, hb * dk), lambda b, h, c: (0, ng + h)),
                  pl.BlockSpec((1, hb * dk), lambda b, h, c: (0, h)),
                  pl.BlockSpec((1, hb * dk), lambda b, h, c: (0, ng + h)),
                  pl.BlockSpec((chunk, hb * dv), rmap(lambda h: h)),
                  gate, gate,
                  pl.BlockSpec((None, hb, dv, dk), smap),
                  pl.BlockSpec((None, hb, 1, dk), smap),
                  pl.BlockSpec((chunk, hb * dv), rmap(lambda h: ng + h)),
                  pl.BlockSpec((chunk, hb * dv), rmap(lambda h: 2 * ng + h)),
                  pl.BlockSpec((1, hb * dv), lambda b, h, c: (0, h))],
        out_specs=[pl.BlockSpec((chunk, hb * dv), lambda b, h, c: (b * nc + c, h)),
                   pl.BlockSpec((None, hb, dv, dk), lambda b, h, c: (b, h, 0, 0)),
                   pl.BlockSpec((None, hb, 1, dk), lambda b, h, c: (b, h, 0, 0)),
                   pl.BlockSpec((None, hb, 1, LANES), lambda b, h, c: (b, h, 0, 0))],
        out_shape=[jax.ShapeDtypeStruct((rows, heads * dv), out_dtype),
                   jax.ShapeDtypeStruct((bsz, heads, dv, dk), f32),
                   jax.ShapeDtypeStruct((bsz, heads, 1, dk), f32),
                   jax.ShapeDtypeStruct((bsz, heads, 1, LANES), f32)],
        scratch_shapes=[pltpu.VMEM((chunk + SUBLANES, hb * dk), f32), pltpu.VMEM((chunk + SUBLANES, hb * dk), f32),
                        pltpu.VMEM((hb, dv, dk), f32), pltpu.VMEM((hb, 1, dk), f32), pltpu.VMEM((hb, 1, LANES), f32)],
        compiler_params=_params("arbitrary", "arbitrary", "arbitrary"),
        name=name,
    )(m0, mqk, mqk, prev, prev, conv_w, conv_w, conv_b, conv_b, u5, gi, gf, c0, n0, u5, u5, hg)


def _rms(x, g):
    return x * lax.rsqrt(jnp.mean(x * x, axis=-1, keepdims=True) + EPS) * g


def _gates(g, b_i, b_f, heads, bsz, chunk):
    gi = g[:, :heads] + b_i
    gf = jax.nn.log_sigmoid(g[:, heads:2 * heads] + b_f)

    def lay(a):
        a = a.reshape(bsz, -1, heads).transpose(0, 2, 1)
        return a.reshape(bsz, heads, -1, 1, chunk)
    return lay(gi), lay(gf)


def kernel(x_prompt, x_sample, cache_k, cache_v, page_table, state_conv, state_C, state_n, state_m, meta, norm_g,
           w_in, b_i, b_f, conv_w, conv_b, lam_q1, lam_k1, lam_q2, lam_k2, subln_g, head_g, w_pa, w_pm, w_out,
           norm_f):
    bsz, seq, d = x_prompt.shape
    dbsz, t_new, _ = x_sample.shape
    depth, n_phys, page, h_a, hd_a = cache_k.shape
    dh_a = hd_a // 2
    w_a = h_a * hd_a
    h_m, dv_m, dk_m = state_C.shape[2], state_C.shape[3], state_C.shape[4]
    w_qk, w_m = h_m * dk_m, h_m * dv_m
    n_meta = meta.shape[0]
    taps = conv_w.shape[1]
    n_s = dbsz * t_new
    mr = bsz * seq

    o_q, o_kv, o_z, o_mqk, o_u5 = 0, w_a, 3 * w_a, 4 * w_a, 4 * w_a + 2 * w_qk
    c_gate = o_u5 + 3 * w_m
    w_in = jnp.swapaxes(w_in, 1, 2)
    w_gate = jnp.pad(w_in[:, c_gate:c_gate + 2 * h_m], ((0, 0), (0, LANES - 2 * h_m), (0, 0)))
    w_g = w_in[:, c_gate + 2 * h_m:]
    wpa, wpm, wo = w_pa.astype(bf16), w_pm.astype(bf16), w_out.astype(bf16)
    g_next = jnp.concatenate([norm_g[1:], norm_f[None]], axis=0).reshape(depth, 1, d)
    conv_b2 = conv_b.reshape(depth, 1, 2 * w_qk)

    groups = h_a // SUBLANES
    ck = cache_k.reshape(depth, n_phys, page, groups, SUBLANES, hd_a)
    cv = cache_v.reshape(depth, n_phys, page, groups, SUBLANES, hd_a)

    x_m = x_prompt.reshape(mr, d)
    x_s = jnp.concatenate([x_sample.reshape(n_s, d), meta.astype(f32)], axis=0)
    xn_m = _rms(x_m, norm_g[0]).astype(bf16)
    xn_s = _rms(x_s, norm_g[0]).astype(bf16)
    chunk = 256 if seq % 256 == 0 else seq
    tq = 256 if seq % 256 == 0 else seq
    qscale = dh_a ** -0.5

    outs = {k: [] for k in ("ks", "vs", "cp", "cs", "Cp", "np", "mp", "Cs", "ns", "ms")}
    kp_buf = vp_buf = None
    for l in range(depth):
        lam_init = 0.8 - 0.6 * math.exp(-0.3 * l)
        lam = (jnp.exp(jnp.sum(lam_q1[l] * lam_k1[l])) - jnp.exp(jnp.sum(lam_q2[l] * lam_k2[l]))
               + lam_init).reshape(1).astype(f32)
        out_scale = 1.0 - lam_init
        sg = subln_g[l].reshape(1, hd_a)
        hg = head_g[l].reshape(1, w_m)
        xn_dtype = bf16 if l + 1 < depth else f32

        mm = functools.partial(_matmul, layer=l, tm=2048, tn=512)
        (q_m,), q_s = mm(f"l{l}_q", xn_m, xn_s, w_in, col0=o_q, ncols=w_a, out_dtypes=[bf16],
                         scale=qscale * LOG2E, side_scale=qscale)
        kvp = functools.partial(_kv_proj, depth=depth, bsz=bsz, n_meta=n_meta, meta_row0=n_s, tm=1024, tn=1024)
        k16_m, kp_buf, k_s = kvp(f"l{l}_k", xn_m, xn_s, w_in, l, o_kv, w_a, kp_buf)
        v16_m, vp_buf, v_s = kvp(f"l{l}_v", xn_m, xn_s, w_in, l, o_kv + w_a, w_a, vp_buf)
        (z_m,), z_s = mm(f"l{l}_z", xn_m, xn_s, w_in, col0=o_z, ncols=w_a, out_dtypes=[bf16])
        (mqk_m,), mqk_s = mm(f"l{l}_mqk", xn_m, xn_s, w_in, col0=o_mqk, ncols=2 * w_qk, out_dtypes=[f32])
        (u5_m,), u5_s = mm(f"l{l}_u5", xn_m, xn_s, w_in, col0=o_u5, ncols=3 * w_m, out_dtypes=[bf16])
        (g_m,), g_s = mm(f"l{l}_g", xn_m, xn_s, w_g, col0=0, ncols=2 * d, out_dtypes=[bf16])
        (gt_m,), gt_s = _matmul(f"l{l}_gate", xn_m, xn_s, w_gate, l, 0, LANES, [f32], 2048, LANES)

        mqk_samp = mqk_s[:n_s].reshape(dbsz, t_new, 2 * w_qk)
        outs["cs"].append(jnp.concatenate([state_conv[l], mqk_samp], axis=1)[:, -(taps - 1):])
        gi_samp, gf_samp = _gates(gt_s[:n_s], b_i[l], b_f[l], h_m, dbsz, t_new)
        gi_meta, gf_meta = _gates(gt_s[n_s:], b_i[l], b_f[l], h_m, 1, n_meta)
        hm_samp, c_samp, n_samp, m_samp = _mlstm(
            f"s{l}_mlstm_sample", mqk_s, state_conv[l], conv_w[l], conv_b2[l], u5_s, gi_samp, gf_samp, state_C[l],
            state_n[l].reshape(dbsz, h_m, 1, dk_m), state_m[l].reshape(-1), hg,
            bsz=dbsz, heads=h_m, chunk=t_new, row0=0, out_dtype=f32)
        hm_meta, c_meta, n_meta_s, m_meta = _mlstm(
            f"s{l}_mlstm_meta", mqk_s, jnp.zeros((1, taps - 1, 2 * w_qk), f32), conv_w[l], conv_b2[l], u5_s,
            gi_meta, gf_meta, jnp.zeros((1, h_m, dv_m, dk_m), f32), jnp.zeros((1, h_m, 1, dk_m), f32),
            jnp.zeros((h_m,), f32), hg, bsz=1, heads=h_m, chunk=n_meta, row0=n_s, out_dtype=f32)
        outs["Cs"].append(c_samp)
        outs["ns"].append(n_samp.reshape(dbsz, h_m, dk_m))
        outs["ms"].append(m_samp[:, :, 0, 0])

        qs = q_s[:n_s].reshape(dbsz, t_new, h_a, hd_a).transpose(0, 2, 1, 3)
        lane = jnp.arange(hd_a)
        qs = jnp.concatenate([jnp.where(lane < dh_a, qs, 0.0), jnp.where(lane >= dh_a, qs, 0.0)], axis=2).astype(bf16)

        def new_page(a):
            a = a.reshape(dbsz, t_new, groups, SUBLANES, hd_a)
            return jnp.pad(a, ((0, 0), (0, page - t_new), (0, 0), (0, 0), (0, 0)))
        att_samp = _paged_attention(f"s{l}_paged_attn", page_table, lam, qs, ck, cv, new_page(k_s[:n_s]),
                                    new_page(v_s[:n_s]), z_s, sg, layer=l, t_new=t_new,
                                    pages_per_step=8 if page_table.shape[1] % 8 == 0 else 4,
                                    out_scale=out_scale)
        att_meta = _meta_attention(f"s{l}_meta_attn", lam, q_s, k_s, v_s, z_s, sg, heads=h_a, n_meta=n_meta,
                                   row0=n_s, out_scale=out_scale)
        att_s = jnp.concatenate([att_samp, att_meta], axis=0)
        hm_s = jnp.concatenate([hm_samp, hm_meta], axis=0)
        x_s, xn_s = _out_proj(f"s{l}_out", att_s, hm_s, g_s, x_s, wpa, wpm, wo, g_next[l], l, 256, xn_dtype)
        outs["ks"].append(k_s[:n_s].reshape(dbsz, t_new, h_a, hd_a))
        outs["vs"].append(v_s[:n_s].reshape(dbsz, t_new, h_a, hd_a))

        outs["cp"].append(mqk_m.reshape(bsz, seq, 2 * w_qk)[:, -(taps - 1):])
        gi_m, gf_m = _gates(gt_m, b_i[l], b_f[l], h_m, bsz, chunk)
        prev_m = mqk_s[n_s + n_meta - (taps - 1):].reshape(1, taps - 1, 2 * w_qk)
        hm_m, c_p, n_p, m_p = _mlstm(
            f"m{l}_mlstm", mqk_m, prev_m, conv_w[l], conv_b2[l], u5_m, gi_m, gf_m, c_meta, n_meta_s,
            m_meta[:, :, 0, 0].reshape(-1), hg, bsz=bsz, heads=h_m, chunk=chunk, row0=0, out_dtype=bf16)
        outs["Cp"].append(c_p)
        outs["np"].append(n_p.reshape(bsz, h_m, dk_m))
        outs["mp"].append(m_p[:, :, 0, 0])

        km = k_s[n_s:].astype(bf16)
        vmt = v_s[n_s:].T.astype(bf16)
        att_m = _flash_attention(f"m{l}_flash_attn", lam, q_m, k16_m, v16_m, km, vmt, z_m, sg, bsz=bsz, heads=h_a,
                                 tq=tq, hb=4, out_scale=out_scale)
        x_m, xn_m = _out_proj(f"m{l}_out", att_m, hm_m, g_m, x_m, wpa, wpm, wo, g_next[l], l, 256, xn_dtype)

    y_prompt = xn_m.reshape(bsz, seq, d)
    y_sample = xn_s[:n_s].reshape(dbsz, t_new, d)
    st = {k: jnp.stack(v) for k, v in outs.items()}
    kp = kp_buf.reshape(depth, bsz, n_meta + seq, h_a, hd_a)
    vp = vp_buf.reshape(depth, bsz, n_meta + seq, h_a, hd_a)
    return (y_prompt, y_sample, kp, vp, st["ks"], st["vs"], st["cp"], st["cs"],
            st["Cp"], st["np"], st["mp"], st["Cs"], st["ns"], st["ms"])
```

```python
import functools
import math

import jax
import jax.numpy as jnp
from jax import lax
from jax.experimental import pallas as pl
from jax.experimental.pallas import tpu as pltpu

f32 = jnp.float32
bf16 = jnp.bfloat16

EPS = 1e-6
LANES = 128
SUBLANES = 8
ONES_ROWS = 16
LOG2E = math.log2(math.e)
V7X_VMEM_LIMIT = 56 * 1024 * 1024
NT = (((1,), (1,)), ((), ()))
TN = (((0,), (0,)), ((), ()))


def _params(*sem):
    return pltpu.CompilerParams(dimension_semantics=sem, vmem_limit_bytes=V7X_VMEM_LIMIT)


def _mm_kernel(x_ref, xs_ref, w_ref, *refs, scale, side_scale):
    o_refs, os_ref, wb_sc = refs[:-2], refs[-2], refs[-1]

    @pl.when(pl.program_id(1) == 0)
    def _():
        wb_sc[...] = w_ref[...].astype(bf16)
        os_ref[...] = lax.dot_general(xs_ref[...], wb_sc[...], NT, preferred_element_type=f32) * side_scale

    acc = lax.dot_general(x_ref[...], wb_sc[...], NT, preferred_element_type=f32)
    if scale != 1.0:
        acc = acc * scale
    for o_ref in o_refs:
        o_ref[...] = acc.astype(o_ref.dtype)


def _matmul(name, x, xs, w, layer, col0, ncols, out_dtypes, tm, tn, scale=1.0, side_scale=1.0):
    m, k = x.shape
    ms = xs.shape[0]
    tm = min(tm, m)
    tn = min(tn, ncols)
    assert col0 % tn == 0 and ncols % tn == 0 and m % tm == 0
    res = pl.pallas_call(
        functools.partial(_mm_kernel, scale=scale, side_scale=side_scale),
        grid=(ncols // tn, m // tm),
        in_specs=[pl.BlockSpec((tm, k), lambda j, i: (i, 0)),
                  pl.BlockSpec((ms, k), lambda j, i: (0, 0)),
                  pl.BlockSpec((None, tn, k), lambda j, i: (layer, col0 // tn + j, 0))],
        out_specs=([pl.BlockSpec((tm, tn), lambda j, i: (i, j)) for _ in out_dtypes]
                   + [pl.BlockSpec((ms, tn), lambda j, i: (0, j))]),
        out_shape=([jax.ShapeDtypeStruct((m, ncols), dt) for dt in out_dtypes]
                   + [jax.ShapeDtypeStruct((ms, ncols), f32)]),
        scratch_shapes=[pltpu.VMEM((tn, k), bf16)],
        compiler_params=_params("arbitrary", "arbitrary"),
        name=name,
    )(x, xs, w)
    return res[:-1], res[-1]


def _store_rows_of_head(ref, hh, row0, val):
    ntok, _, hd = ref.shape
    ref.reshape(ntok * SUBLANES, hd)[pl.ds(row0 * SUBLANES + hh, val.shape[0], stride=SUBLANES), :] = val


def _kv_kernel(x_ref, xs_ref, w_ref, *refs, nt, shift, meta_row0, aliased):
    if aliased:
        refs = refs[1:]
    o16_ref, out_ref, os_ref, wb_sc, meta_sc, carry_sc = refs
    b, i = pl.program_id(1), pl.program_id(2)
    tm = x_ref.shape[0]

    @pl.when((b == 0) & (i == 0))
    def _():
        wb_sc[...] = w_ref[...].astype(bf16)
        side = lax.dot_general(xs_ref[...], wb_sc[...], NT, preferred_element_type=f32)
        os_ref[...] = side
        meta_sc[...] = side[meta_row0:meta_row0 + shift]

    @pl.when(i == 0)
    def _():
        carry_sc[...] = meta_sc[...]

    hd = out_ref.shape[2]

    def put(row0, val):
        for hh in range(SUBLANES):
            _store_rows_of_head(out_ref, hh, row0, val[:, hh * hd:(hh + 1) * hd])

    @pl.when(i < nt)
    def _():
        res = lax.dot_general(x_ref[...], wb_sc[...], NT, preferred_element_type=f32)
        o16_ref[...] = res.astype(o16_ref.dtype)
        put(0, carry_sc[...])
        put(shift, res[:tm - shift])
        carry_sc[...] = res[tm - shift:]

    @pl.when(i == nt)
    def _():
        out_ref[shift:] = jnp.zeros((tm - shift,) + out_ref.shape[1:], f32)
        put(0, carry_sc[...])


def _kv_proj(name, x, xs, w, layer, col0, ncols, buf, *, depth, bsz, n_meta, meta_row0, tm, tn):
    m, k = x.shape
    ms = xs.shape[0]
    seq = m // bsz
    tm = min(tm, seq)
    nt = seq // tm
    assert seq % tm == 0 and col0 % tn == 0 and ncols % tn == 0 and n_meta % SUBLANES == 0 and n_meta < tm
    hd = tn // SUBLANES
    assert hd % LANES == 0
    aliased = buf is not None
    kern = functools.partial(_kv_kernel, nt=nt, shift=n_meta, meta_row0=meta_row0, aliased=aliased)

    def xmap(j, b, i):
        return (b * nt + jnp.minimum(i, nt - 1), 0)

    in_specs = [pl.BlockSpec((tm, k), xmap),
                pl.BlockSpec((ms, k), lambda j, b, i: (0, 0)),
                pl.BlockSpec((None, tn, k), lambda j, b, i: (layer, col0 // tn + j, 0))]
    args = [x, xs, w]
    if aliased:
        in_specs.append(pl.BlockSpec(memory_space=pl.ANY))
        args.append(buf)
    o16, out, side = pl.pallas_call(
        kern,
        grid=(ncols // tn, bsz, nt + 1),
        in_specs=in_specs,
        out_specs=[pl.BlockSpec((tm, tn), lambda j, b, i: (b * nt + jnp.minimum(i, nt - 1), j)),
                   pl.BlockSpec((None, None, tm, None, SUBLANES, hd), lambda j, b, i: (layer, b, i, j, 0, 0)),
                   pl.BlockSpec((ms, tn), lambda j, b, i: (0, j))],
        out_shape=[jax.ShapeDtypeStruct((m, ncols), bf16),
                   jax.ShapeDtypeStruct((depth, bsz, n_meta + seq, ncols // tn, SUBLANES, hd), f32),
                   jax.ShapeDtypeStruct((ms, ncols), f32)],
        scratch_shapes=[pltpu.VMEM((tn, k), bf16), pltpu.VMEM((n_meta, tn), f32), pltpu.VMEM((n_meta, tn), f32)],
        input_output_aliases={3: 1} if aliased else {},
        compiler_params=_params("arbitrary", "arbitrary", "arbitrary"),
        name=name,
    )(*args)
    return o16, out, side


def _out_kernel(a_ref, m_ref, ga_ref, gm_ref, x_ref, wpa_ref, wpm_ref, wo_ref, gn_ref, o_ref, xn_ref):
    ya = jnp.dot(a_ref[...].astype(bf16), wpa_ref[...], preferred_element_type=f32)
    ym = jnp.dot(m_ref[...].astype(bf16), wpm_ref[...], preferred_element_type=f32)
    merged = (jax.nn.sigmoid(ga_ref[...].astype(f32)) * ya
              + jax.nn.sigmoid(gm_ref[...].astype(f32)) * ym)
    x_new = x_ref[...] + jnp.dot(merged.astype(bf16), wo_ref[...], preferred_element_type=f32)
    o_ref[...] = x_new
    xn = x_new * lax.rsqrt(jnp.mean(x_new * x_new, axis=-1, keepdims=True) + EPS) * gn_ref[...]
    xn_ref[...] = xn.astype(xn_ref.dtype)


def _out_proj(name, a, mm, g, x, w_pa, w_pm, w_out, g_next, layer, tm, xn_dtype):
    m, d = x.shape
    tm = min(tm, m)
    wspec = pl.BlockSpec((None, d, d), lambda i: (layer, 0, 0), pipeline_mode=pl.Buffered(1))
    row = pl.BlockSpec((tm, d), lambda i: (i, 0))
    return pl.pallas_call(
        _out_kernel,
        grid=(pl.cdiv(m, tm),),
        in_specs=[row, row, row, pl.BlockSpec((tm, d), lambda i: (i, 1)), row, wspec, wspec, wspec,
                  pl.BlockSpec((1, d), lambda i: (0, 0))],
        out_specs=[row, row],
        out_shape=[jax.ShapeDtypeStruct((m, d), f32), jax.ShapeDtypeStruct((m, d), xn_dtype)],
        compiler_params=_params("arbitrary"),
        name=name,
    )(a, mm, g, g, x, w_pa, w_pm, w_out, g_next)


def _subln_gate(o, z, g, scale):
    y = o * lax.rsqrt(jnp.mean(o * o, axis=-1, keepdims=True) + EPS) * (g * scale)
    zf = z.astype(f32)
    return y * (zf * jax.nn.sigmoid(zf))


def _stack_maps(q, dh):
    lane = lax.broadcasted_iota(jnp.int32, q.shape, 1)
    zero = jnp.zeros_like(q)
    return jnp.concatenate([jnp.where(lane < dh, q, zero), jnp.where(lane >= dh, q, zero)], axis=0)


def _flash_kernel(lam_ref, q_ref, k_ref, v_ref, km_ref, vmt_ref, z_ref, g_ref, o_ref, qs_sc, vt_sc, m_sc, acc_sc,
                  *, tq, hd, hb, out_scale):
    i = pl.program_id(2)
    tr = k_ref.shape[0]
    dh = hd // 2
    ones = jnp.ones((ONES_ROWS, tq), bf16)

    @pl.when(i == 0)
    def _():
        for hh in range(hb):
            for c in range(tr // tq):
                vt = v_ref[c * tq:(c + 1) * tq, hh * hd:(hh + 1) * hd].astype(f32).T.astype(bf16)
                vt_sc[hh, :, c * tq:(c + 1) * tq] = jnp.concatenate([vt, ones], axis=0)

    def scores_t(hh, kb):
        return lax.dot_general(kb, qs_sc[hh], NT, preferred_element_type=f32)

    hs = range(hb)
    n_meta = km_ref.shape[0]
    head_cols = [slice(hh * hd, (hh + 1) * hd) for hh in hs]
    for hh in hs:
        qs_sc[hh] = _stack_maps(q_ref[:, head_cols[hh]], dh)

    off_d = pl.multiple_of(i * tq, tq)
    key = lax.broadcasted_iota(jnp.int32, (n_meta + tq, 2 * tq), 0)
    qry = lax.broadcasted_iota(jnp.int32, (n_meta + tq, 2 * tq), 1)
    visible = key - n_meta <= jnp.where(qry >= tq, qry - tq, qry)
    ss = [jnp.where(visible, scores_t(hh, jnp.concatenate([km_ref[:, head_cols[hh]],
                                                           k_ref[pl.ds(off_d, tq), head_cols[hh]]], axis=0)),
                    -jnp.inf) for hh in hs]
    ms = [jnp.max(ss[hh], axis=0, keepdims=True) for hh in hs]
    ps = [jnp.exp2(ss[hh] - ms[hh]).astype(bf16) for hh in hs]
    ones_m = jnp.ones((ONES_ROWS, n_meta), bf16)
    for hh in hs:
        vmt = jnp.concatenate([vmt_ref[head_cols[hh], :], ones_m], axis=0)
        acc_sc[hh] = (jnp.dot(vmt, ps[hh][:n_meta], preferred_element_type=f32)
                      + jnp.dot(vt_sc[hh, :, pl.ds(off_d, tq)], ps[hh][n_meta:], preferred_element_type=f32))
        m_sc[hh] = ms[hh]

    def update(off, nkeys):
        ss = [scores_t(hh, k_ref[pl.ds(off, nkeys), head_cols[hh]]) for hh in hs]
        m_prev = [m_sc[hh] for hh in hs]
        m_new = [jnp.maximum(m_prev[hh], jnp.max(ss[hh], axis=0, keepdims=True)) for hh in hs]
        ps = [jnp.exp2(ss[hh] - m_new[hh]).astype(bf16) for hh in hs]
        pvs = [jnp.dot(vt_sc[hh, :, pl.ds(off, nkeys)], ps[hh], preferred_element_type=f32) for hh in hs]
        for hh in hs:
            acc_sc[hh] = jnp.exp2(m_prev[hh] - m_new[hh]) * acc_sc[hh] + pvs[hh]
            m_sc[hh] = m_new[hh]

    def body(j, carry):
        update(pl.multiple_of(j * (2 * tq), 2 * tq), 2 * tq)
        return carry

    lax.fori_loop(0, i // 2, body, 0)

    @pl.when(i % 2 == 1)
    def _():
        update(pl.multiple_of((i - 1) * tq, tq), tq)

    for hh in range(hb):
        cols = slice(hh * hd, (hh + 1) * hd)
        acc = acc_sc[hh]
        o = acc[:hd] * (1.0 / acc[hd:hd + 1])
        out = (o[:, :tq] - lam_ref[0] * o[:, tq:]).T
        o_ref[:, cols] = _subln_gate(out, z_ref[:, cols], g_ref[...], out_scale).astype(o_ref.dtype)


def _flash_attention(name, lam, q, k, v, km, vmt, z, g, *, bsz, heads, tq, hb, out_scale):
    mr, w = q.shape
    tr = mr // bsz
    nq = tr // tq
    hd = w // heads
    n_meta = km.shape[0]
    ng = heads // hb
    kern = functools.partial(_flash_kernel, tq=tq, hd=hd, hb=hb, out_scale=out_scale)
    qspec = pl.BlockSpec((tq, hb * hd), lambda b, h, i: (b * nq + i, h))
    return pl.pallas_call(
        kern,
        grid=(bsz, ng, nq),
        in_specs=[pl.BlockSpec(memory_space=pltpu.SMEM),
                  qspec,
                  pl.BlockSpec((tr, hb * hd), lambda b, h, i: (b, h)),
                  pl.BlockSpec((tr, hb * hd), lambda b, h, i: (b, h)),
                  pl.BlockSpec((n_meta, hb * hd), lambda b, h, i: (0, h)),
                  pl.BlockSpec((hb * hd, n_meta), lambda b, h, i: (h, 0)),
                  qspec,
                  pl.BlockSpec((1, hd), lambda b, h, i: (0, 0))],
        out_specs=qspec,
        out_shape=jax.ShapeDtypeStruct((mr, w), bf16),
        scratch_shapes=[pltpu.VMEM((hb, 2 * tq, hd), bf16), pltpu.VMEM((hb, hd + ONES_ROWS, tr), bf16),
                        pltpu.VMEM((hb, 1, 2 * tq), f32), pltpu.VMEM((hb, hd + ONES_ROWS, 2 * tq), f32)],
        compiler_params=_params("arbitrary", "arbitrary", "arbitrary"),
        name=name,
    )(lam, q, k, v, km, vmt, z, g)


def _meta_attn_kernel(lam_ref, q_ref, k_ref, v_ref, z_ref, g_ref, o_ref, *, dh, out_scale):
    r = q_ref.shape[0]
    qs = _stack_maps(q_ref[...].astype(bf16), dh)
    s = lax.dot_general(qs, k_ref[...].astype(bf16), NT, preferred_element_type=f32)
    row = lax.broadcasted_iota(jnp.int32, s.shape, 0)
    row = jnp.where(row >= r, row - r, row)
    col = lax.broadcasted_iota(jnp.int32, s.shape, 1)
    s = jnp.where(col <= row, s, -jnp.inf)
    p = jnp.exp(s - jnp.max(s, axis=-1, keepdims=True))
    l = jnp.sum(p, axis=-1, keepdims=True)
    o = jnp.dot(p.astype(bf16), v_ref[...].astype(bf16), preferred_element_type=f32) * (1.0 / l)
    out = o[:r] - lam_ref[0] * o[r:]
    o_ref[...] = _subln_gate(out, z_ref[...], g_ref[...], out_scale).astype(o_ref.dtype)


def _meta_attention(name, lam, q, k, v, z, g, *, heads, n_meta, row0, out_scale):
    hd = q.shape[1] // heads
    rb = row0 // n_meta
    kern = functools.partial(_meta_attn_kernel, dh=hd // 2, out_scale=out_scale)
    blk = pl.BlockSpec((n_meta, hd), lambda h: (rb, h))
    return pl.pallas_call(
        kern,
        grid=(heads,),
        in_specs=[pl.BlockSpec(memory_space=pltpu.SMEM), blk, blk, blk, blk,
                  pl.BlockSpec((1, hd), lambda h: (0, 0))],
        out_specs=pl.BlockSpec((n_meta, hd), lambda h: (0, h)),
        out_shape=jax.ShapeDtypeStruct((n_meta, q.shape[1]), f32),
        compiler_params=_params("arbitrary"),
        name=name,
    )(lam, q, k, v, z, g)


def _rows_of_head(ref, hh):
    ntok, _, hd = ref.shape
    return ref.reshape(ntok * SUBLANES, hd)[pl.ds(hh, ntok, stride=SUBLANES), :]


def _paged_kernel(pt_ref, lam_ref, qs_ref, *refs, n_steps, pages_per_step, groups, t_new, out_scale):
    del pt_ref
    npg = pages_per_step * groups
    k_refs, v_refs = refs[:npg], refs[npg:2 * npg]
    kn_refs, vn_refs = refs[2 * npg:2 * npg + groups], refs[2 * npg + groups:2 * npg + 2 * groups]
    z_ref, g_ref, o_ref, m_sc, l_sc, acc_sc = refs[2 * npg + 2 * groups:]
    step = pl.program_id(1)
    rows = 2 * t_new
    heads = groups * SUBLANES
    hd = qs_ref.shape[-1]

    @pl.when(step == 0)
    def _():
        m_sc[...] = jnp.full(m_sc.shape, -jnp.inf, f32)
        l_sc[...] = jnp.zeros(l_sc.shape, f32)
        acc_sc[...] = jnp.zeros(acc_sc.shape, f32)

    def head_rows(ref_list, h):
        return _rows_of_head(ref_list[h // SUBLANES], h % SUBLANES).astype(bf16)

    def process(kpages, vpages, causal):
        ntok = kpages[0][0].shape[0]
        s = jnp.concatenate(
            [jnp.concatenate([lax.dot_general(qs_ref[h], head_rows(kg, h), NT, preferred_element_type=f32)
                              for kg in kpages], axis=1) for h in range(heads)], axis=0)
        if causal:
            t = lax.broadcasted_iota(jnp.int32, s.shape, 0) % t_new
            col = lax.broadcasted_iota(jnp.int32, s.shape, 1)
            s = jnp.where(col <= t, s, -jnp.inf)
        m_prev = m_sc[...]
        m_new = jnp.maximum(m_prev, jnp.max(s, axis=-1, keepdims=True))
        alpha = jnp.exp(m_prev - m_new)
        p = jnp.exp(s - m_new)
        l_sc[...] = alpha * l_sc[...] + jnp.sum(p, axis=-1, keepdims=True)
        m_sc[...] = m_new
        pb = p.astype(bf16)
        pv = jnp.concatenate(
            [sum(jnp.dot(pb[h * rows:(h + 1) * rows, j * ntok:(j + 1) * ntok], head_rows(vg, h),
                         preferred_element_type=f32) for j, vg in enumerate(vpages))
             for h in range(heads)], axis=0)
        acc_sc[...] = alpha * acc_sc[...] + pv

    @pl.when(step < n_steps)
    def _():
        process([k_refs[j * groups:(j + 1) * groups] for j in range(pages_per_step)],
                [v_refs[j * groups:(j + 1) * groups] for j in range(pages_per_step)], False)

    @pl.when(step == n_steps)
    def _():
        process([kn_refs], [vn_refs], True)
        o = acc_sc[...] * (1.0 / l_sc[...])
        lam = lam_ref[0]
        for h in range(heads):
            oh = o[h * rows:h * rows + t_new] - lam * o[h * rows + t_new:(h + 1) * rows]
            cols = slice(h * hd, (h + 1) * hd)
            o_ref[:, cols] = _subln_gate(oh, z_ref[:, cols], g_ref[...], out_scale).astype(o_ref.dtype)


def _paged_attention(name, page_table, lam, qs, cache_k, cache_v, kn, vn, z, g, *, layer, t_new, pages_per_step,
                     out_scale):
    dbsz, n_pages = page_table.shape
    pp = pages_per_step
    assert n_pages % pp == 0
    n_steps = n_pages // pp
    page, groups, hd = cache_k.shape[2], cache_k.shape[3], cache_k.shape[5]
    heads = groups * SUBLANES
    w = heads * hd
    rows = 2 * t_new

    def page_spec(j, gidx):
        def imap(b, s, pt):
            return (layer, pt[b, jnp.minimum(s, n_steps - 1) * pp + j], 0, gidx, 0, 0)
        return pl.BlockSpec((None, None, page, None, SUBLANES, hd), imap)

    def new_spec(gidx):
        return pl.BlockSpec((None, kn.shape[1], None, SUBLANES, hd), lambda b, s, pt: (b, 0, gidx, 0, 0))

    page_specs = [page_spec(j, gidx) for j in range(pp) for gidx in range(groups)]
    new_specs = [new_spec(gidx) for gidx in range(groups)]
    kern = functools.partial(_paged_kernel, n_steps=n_steps, pages_per_step=pp, groups=groups, t_new=t_new,
                             out_scale=out_scale)
    grid_spec = pltpu.PrefetchScalarGridSpec(
        num_scalar_prefetch=1,
        grid=(dbsz, n_steps + 1),
        in_specs=([pl.BlockSpec(memory_space=pltpu.SMEM),
                   pl.BlockSpec((None, heads, rows, hd), lambda b, s, pt: (b, 0, 0, 0))]
                  + page_specs + page_specs + new_specs + new_specs
                  + [pl.BlockSpec((t_new, w), lambda b, s, pt: (b, 0)),
                     pl.BlockSpec((1, hd), lambda b, s, pt: (0, 0))]),
        out_specs=pl.BlockSpec((t_new, w), lambda b, s, pt: (b, 0)),
        scratch_shapes=[pltpu.VMEM((heads * rows, 1), f32), pltpu.VMEM((heads * rows, 1), f32),
                        pltpu.VMEM((heads * rows, hd), f32)],
    )
    return pl.pallas_call(
        kern,
        grid_spec=grid_spec,
        out_shape=jax.ShapeDtypeStruct((dbsz * t_new, w), f32),
        compiler_params=_params("arbitrary", "arbitrary"),
        name=name,
    )(page_table, lam, qs, *([cache_k] * (pp * groups)), *([cache_v] * (pp * groups)),
      *([kn] * groups), *([vn] * groups), z, g)


def _mlstm_kernel(m0_ref, uq_ref, uk_ref, pq_ref, pk_ref, wq_ref, wk_ref, bq_ref, bk_ref, v_ref, gi_ref, gf_ref,
                  c0_ref, n0_ref, og_ref, zg_ref, hg_ref, h_ref, c_out, n_out, m_out,
                  padq_sc, padk_sc, c_sc, n_sc, m_sc, *, heads, hb, shared_init, q_scale):
    b, hgrp, c = pl.program_id(0), pl.program_id(1), pl.program_id(2)
    chunk = uq_ref.shape[0]
    taps = wq_ref.shape[0]
    dv, dk = c_sc.shape[1], c_sc.shape[2]
    hs = range(hb)
    base = SUBLANES - (taps - 1)

    @pl.when(c == 0)
    def _():
        c_sc[...] = c0_ref[...]
        n_sc[...] = n0_ref[...]
        for hh in hs:
            m0 = m0_ref[hgrp * hb + hh if shared_init else b * heads + hgrp * hb + hh]
            m_sc[hh] = jnp.full((1, LANES), m0, f32)
        padq_sc[base:SUBLANES, :] = pq_ref[...]
        padk_sc[base:SUBLANES, :] = pk_ref[...]

    def conv_silu(u_ref, w_ref, b_ref, pad_sc):
        pad_sc[SUBLANES:SUBLANES + chunk, :] = u_ref[...]
        out = b_ref[...] + pad_sc[base:base + chunk, :] * w_ref[0:1, :]
        for j in range(1, taps):
            out = out + pad_sc[base + j:base + j + chunk, :] * w_ref[j:j + 1, :]
        pad_sc[base:SUBLANES, :] = pad_sc[base + chunk:SUBLANES + chunk, :]
        return out * jax.nn.sigmoid(out)

    q_all = (conv_silu(uq_ref, wq_ref, bq_ref, padq_sc) * q_scale).astype(bf16)
    k_all = conv_silu(uk_ref, wk_ref, bk_ref, padk_sc).astype(bf16)
    q = [q_all[:, hh * dk:(hh + 1) * dk] for hh in hs]
    k = [k_all[:, hh * dk:(hh + 1) * dk] for hh in hs]
    v = [v_ref[:, hh * dv:(hh + 1) * dv].astype(bf16) for hh in hs]
    c_prev = [c_sc[hh] for hh in hs]
    n_prev = [n_sc[hh] for hh in hs]
    m_prev = [m_sc[hh][:, :1] for hh in hs]
    qk = [lax.dot_general(q[hh], k[hh], NT, preferred_element_type=f32) for hh in hs]
    qc = [lax.dot_general(q[hh], c_prev[hh].astype(bf16), NT, preferred_element_type=f32) for hh in hs]

    li = [gi_ref[hh] for hh in hs]
    lf = [gf_ref[hh] for hh in hs]
    r = lax.broadcasted_iota(jnp.int32, (chunk, chunk), 0)
    s = lax.broadcasted_iota(jnp.int32, (chunk, chunk), 1)
    tri, eye = s <= r, s == r
    lf_col = [jnp.sum(jnp.where(eye, lf[hh], 0.0), axis=1, keepdims=True) for hh in hs]
    li_col = [jnp.sum(jnp.where(eye, li[hh], 0.0), axis=1, keepdims=True) for hh in hs]
    b_col = [jnp.sum(jnp.where(tri, lf[hh], 0.0), axis=1, keepdims=True) for hh in hs]
    b_row = [jnp.sum(jnp.where(r <= s, lf_col[hh], 0.0), axis=0, keepdims=True) for hh in hs]
    a_col = [b_col[hh] + m_prev[hh] for hh in hs]
    d = [jnp.where(tri, b_col[hh] - b_row[hh] + li[hh], -jnp.inf) for hh in hs]
    mt = [jnp.maximum(a_col[hh], jnp.max(d[hh], axis=1, keepdims=True)) for hh in hs]
    w_prev = [jnp.exp(a_col[hh] - mt[hh]) for hh in hs]
    sm = [qk[hh] * jnp.exp(d[hh] - mt[hh]) for hh in hs]
    sv = [jnp.dot(sm[hh].astype(bf16), v[hh], preferred_element_type=f32) for hh in hs]

    b_end = [jnp.sum(lf[hh], axis=1, keepdims=True) for hh in hs]
    a_end = [b_end[hh] + m_prev[hh] for hh in hs]
    m_new = [jnp.maximum(a_end[hh], jnp.max(b_end[hh] - b_row[hh] + li[hh], axis=1, keepdims=True)) for hh in hs]
    w_old = [jnp.exp(a_end[hh] - m_new[hh]) for hh in hs]
    w_tok = [jnp.exp(b_end[hh] - b_col[hh] + li_col[hh] - m_new[hh]) for hh in hs]
    vw = [(v[hh].astype(f32) * w_tok[hh]).astype(bf16) for hh in hs]
    vk = [lax.dot_general(vw[hh], k[hh], TN, preferred_element_type=f32) for hh in hs]
    for hh in hs:
        c_sc[hh] = w_old[hh] * c_prev[hh] + vk[hh]
        n_sc[hh] = w_old[hh] * n_prev[hh] + jnp.sum(k[hh].astype(f32) * w_tok[hh], axis=0, keepdims=True)
        m_sc[hh] = jnp.broadcast_to(m_new[hh], (1, LANES))

    for hh in hs:
        num = sv[hh] + w_prev[hh] * qc[hh]
        qn = jnp.sum(q[hh].astype(f32) * n_prev[hh], axis=1, keepdims=True)
        den = jnp.sum(sm[hh], axis=1, keepdims=True) + w_prev[hh] * qn
        hx = num / jnp.maximum(jnp.abs(den), jnp.exp(-mt[hh]))
        cols = slice(hh * dv, (hh + 1) * dv)
        hn = hx * lax.rsqrt(jnp.mean(hx * hx, axis=1, keepdims=True) + EPS) * hg_ref[:, cols]
        zg = zg_ref[:, cols].astype(f32)
        h_ref[:, cols] = (hn * jax.nn.sigmoid(og_ref[:, cols].astype(f32))
                          * (zg * jax.nn.sigmoid(zg))).astype(h_ref.dtype)

    @pl.when(c == pl.num_programs(2) - 1)
    def _():
        c_out[...] = c_sc[...]
        n_out[...] = n_sc[...]
        m_out[...] = m_sc[...]


def _mlstm(name, mqk, prev, conv_w, conv_b, u5, gi, gf, c0, n0, m0, hg, *, bsz, heads, chunk, row0, out_dtype,
           hb=2):
    nc = gi.shape[2]
    dk = mqk.shape[1] // (2 * heads)
    dv = c0.shape[2]
    taps = conv_w.shape[0]
    rb = row0 // chunk
    shared = c0.shape[0] == 1
    shared_prev = prev.shape[0] == 1
    rows = bsz * nc * chunk

    ng = heads // hb

    def rmap(cb):
        return lambda b, h, c: (rb + b * nc + c, cb(h))

    def smap(b, h, c):
        return (0 if shared else b, h, 0, 0)

    def pmap(cb):
        return lambda b, h, c: (0 if shared_prev else b, 0, cb(h))

    gate = pl.BlockSpec((None, hb, None, 1, chunk), lambda b, h, c: (b, h, c, 0, 0))
    kern = functools.partial(_mlstm_kernel, heads=heads, hb=hb, shared_init=shared, q_scale=dk ** -0.5)
    return pl.pallas_call(
        kern,
        grid=(bsz, ng, nc),
        in_specs=[pl.BlockSpec(memory_space=pltpu.SMEM),
                  pl.BlockSpec((chunk, hb * dk), rmap(lambda h: h)),
                  pl.BlockSpec((chunk, hb * dk), rmap(lambda h: ng + h)),
                  pl.BlockSpec((None, taps - 1, hb * dk), pmap(lambda h: h)),
                  pl.BlockSpec((None, taps - 1, hb * dk), pmap(lambda h: ng + h)),
                  pl.BlockSpec((taps, hb * dk), lambda b, h, c: (0, h)),
                  pl.BlockSpec((taps, hb * dk), lambda b, h, c: (0, ng + h)),
                  pl.BlockSpec((1, hb * dk), lambda b, h, c: (0, h)),
                  pl.BlockSpec((1, hb * dk), lambda b, h, c: (0, ng + h)),
                  pl.BlockSpec((chunk, hb * dv), rmap(lambda h: h)),
                  gate, gate,
                  pl.BlockSpec((None, hb, dv, dk), smap),
                  pl.BlockSpec((None, hb, 1, dk), smap),
                  pl.BlockSpec((chunk, hb * dv), rmap(lambda h: ng + h)),
                  pl.BlockSpec((chunk, hb * dv), rmap(lambda h: 2 * ng + h)),
                  pl.BlockSpec((1, hb * dv), lambda b, h, c: (0, h))],
        out_specs=[pl.BlockSpec((chunk, hb * dv), lambda b, h, c: (b * nc + c, h)),
                   pl.BlockSpec((None, hb, dv, dk), lambda b, h, c: (b, h, 0, 0)),
                   pl.BlockSpec((None, hb, 1, dk), lambda b, h, c: (b, h, 0, 0)),
                   pl.BlockSpec((None, hb, 1, LANES), lambda b, h, c: (b, h, 0, 0))],
        out_shape=[jax.ShapeDtypeStruct((rows, heads * dv), out_dtype),
                   jax.ShapeDtypeStruct((bsz, heads, dv, dk), f32),
                   jax.ShapeDtypeStruct((bsz, heads, 1, dk), f32),
                   jax.ShapeDtypeStruct((bsz, heads, 1, LANES), f32)],
        scratch_shapes=[pltpu.VMEM((chunk + SUBLANES, hb * dk), f32), pltpu.VMEM((chunk + SUBLANES, hb * dk), f32),
                        pltpu.VMEM((hb, dv, dk), f32), pltpu.VMEM((hb, 1, dk), f32), pltpu.VMEM((hb, 1, LANES), f32)],
        compiler_params=_params("arbitrary", "arbitrary", "arbitrary"),
        name=name,
    )(m0, mqk, mqk, prev, prev, conv_w, conv_w, conv_b, conv_b, u5, gi, gf, c0, n0, u5, u5, hg)


def _rms(x, g):
    return x * lax.rsqrt(jnp.mean(x * x, axis=-1, keepdims=True) + EPS) * g


def _gates(g, b_i, b_f, heads, bsz, chunk):
    gi = g[:, :heads] + b_i
    gf = jax.nn.log_sigmoid(g[:, heads:2 * heads] + b_f)

    def lay(a):
        a = a.reshape(bsz, -1, heads).transpose(0, 2, 1)
        return a.reshape(bsz, heads, -1, 1, chunk)
    return lay(gi), lay(gf)


def kernel(x_prompt, x_sample, cache_k, cache_v, page_table, state_conv, state_C, state_n, state_m, meta, norm_g,
           w_in, b_i, b_f, conv_w, conv_b, lam_q1, lam_k1, lam_q2, lam_k2, subln_g, head_g, w_pa, w_pm, w_out,
           norm_f):
    bsz, seq, d = x_prompt.shape
    dbsz, t_new, _ = x_sample.shape
    depth, n_phys, page, h_a, hd_a = cache_k.shape
    dh_a = hd_a // 2
    w_a = h_a * hd_a
    h_m, dv_m, dk_m = state_C.shape[2], state_C.shape[3], state_C.shape[4]
    w_qk, w_m = h_m * dk_m, h_m * dv_m
    n_meta = meta.shape[0]
    taps = conv_w.shape[1]
    n_s = dbsz * t_new
    mr = bsz * seq

    o_q, o_kv, o_z, o_mqk, o_u5 = 0, w_a, 3 * w_a, 4 * w_a, 4 * w_a + 2 * w_qk
    c_gate = o_u5 + 3 * w_m
    w_in = jnp.swapaxes(w_in, 1, 2)
    w_gate = jnp.pad(w_in[:, c_gate:c_gate + 2 * h_m], ((0, 0), (0, LANES - 2 * h_m), (0, 0)))
    w_g = w_in[:, c_gate + 2 * h_m:]
    wpa, wpm, wo = w_pa.astype(bf16), w_pm.astype(bf16), w_out.astype(bf16)
    g_next = jnp.concatenate([norm_g[1:], norm_f[None]], axis=0).reshape(depth, 1, d)
    conv_b2 = conv_b.reshape(depth, 1, 2 * w_qk)

    groups = h_a // SUBLANES
    ck = cache_k.reshape(depth, n_phys, page, groups, SUBLANES, hd_a)
    cv = cache_v.reshape(depth, n_phys, page, groups, SUBLANES, hd_a)

    x_m = x_prompt.reshape(mr, d)
    x_s = jnp.concatenate([x_sample.reshape(n_s, d), meta.astype(f32)], axis=0)
    xn_m = _rms(x_m, norm_g[0]).astype(bf16)
    xn_s = _rms(x_s, norm_g[0]).astype(bf16)
    chunk = 256 if seq % 256 == 0 else seq
    tq = next((t for t in (512, 256) if seq % t == 0), seq)
    qscale = dh_a ** -0.5

    outs = {k: [] for k in ("ks", "vs", "cp", "cs", "Cp", "np", "mp", "Cs", "ns", "ms")}
    kp_buf = vp_buf = None
    for l in range(depth):
        lam_init = 0.8 - 0.6 * math.exp(-0.3 * l)
        lam = (jnp.exp(jnp.sum(lam_q1[l] * lam_k1[l])) - jnp.exp(jnp.sum(lam_q2[l] * lam_k2[l]))
               + lam_init).reshape(1).astype(f32)
        out_scale = 1.0 - lam_init
        sg = subln_g[l].reshape(1, hd_a)
        hg = head_g[l].reshape(1, w_m)
        xn_dtype = bf16 if l + 1 < depth else f32

        mm = functools.partial(_matmul, layer=l, tm=2048, tn=512)
        (q_m,), q_s = mm(f"l{l}_q", xn_m, xn_s, w_in, col0=o_q, ncols=w_a, out_dtypes=[bf16],
                         scale=qscale * LOG2E, side_scale=qscale)
        kvp = functools.partial(_kv_proj, depth=depth, bsz=bsz, n_meta=n_meta, meta_row0=n_s, tm=1024, tn=1024)
        k16_m, kp_buf, k_s = kvp(f"l{l}_k", xn_m, xn_s, w_in, l, o_kv, w_a, kp_buf)
        v16_m, vp_buf, v_s = kvp(f"l{l}_v", xn_m, xn_s, w_in, l, o_kv + w_a, w_a, vp_buf)
        (z_m,), z_s = mm(f"l{l}_z", xn_m, xn_s, w_in, col0=o_z, ncols=w_a, out_dtypes=[bf16])
        (mqk_m,), mqk_s = mm(f"l{l}_mqk", xn_m, xn_s, w_in, col0=o_mqk, ncols=2 * w_qk, out_dtypes=[f32])
        (u5_m,), u5_s = mm(f"l{l}_u5", xn_m, xn_s, w_in, col0=o_u5, ncols=3 * w_m, out_dtypes=[bf16])
        (g_m,), g_s = mm(f"l{l}_g", xn_m, xn_s, w_g, col0=0, ncols=2 * d, out_dtypes=[bf16])
        (gt_m,), gt_s = _matmul(f"l{l}_gate", xn_m, xn_s, w_gate, l, 0, LANES, [f32], 2048, LANES)

        mqk_samp = mqk_s[:n_s].reshape(dbsz, t_new, 2 * w_qk)
        outs["cs"].append(jnp.concatenate([state_conv[l], mqk_samp], axis=1)[:, -(taps - 1):])
        gi_samp, gf_samp = _gates(gt_s[:n_s], b_i[l], b_f[l], h_m, dbsz, t_new)
        gi_meta, gf_meta = _gates(gt_s[n_s:], b_i[l], b_f[l], h_m, 1, n_meta)
        hm_samp, c_samp, n_samp, m_samp = _mlstm(
            f"s{l}_mlstm_sample", mqk_s, state_conv[l], conv_w[l], conv_b2[l], u5_s, gi_samp, gf_samp, state_C[l],
            state_n[l].reshape(dbsz, h_m, 1, dk_m), state_m[l].reshape(-1), hg,
            bsz=dbsz, heads=h_m, chunk=t_new, row0=0, out_dtype=f32)
        hm_meta, c_meta, n_meta_s, m_meta = _mlstm(
            f"s{l}_mlstm_meta", mqk_s, jnp.zeros((1, taps - 1, 2 * w_qk), f32), conv_w[l], conv_b2[l], u5_s,
            gi_meta, gf_meta, jnp.zeros((1, h_m, dv_m, dk_m), f32), jnp.zeros((1, h_m, 1, dk_m), f32),
            jnp.zeros((h_m,), f32), hg, bsz=1, heads=h_m, chunk=n_meta, row0=n_s, out_dtype=f32)
        outs["Cs"].append(c_samp)
        outs["ns"].append(n_samp.reshape(dbsz, h_m, dk_m))
        outs["ms"].append(m_samp[:, :, 0, 0])

        qs = q_s[:n_s].reshape(dbsz, t_new, h_a, hd_a).transpose(0, 2, 1, 3)
        lane = jnp.arange(hd_a)
        qs = jnp.concatenate([jnp.where(lane < dh_a, qs, 0.0), jnp.where(lane >= dh_a, qs, 0.0)], axis=2).astype(bf16)

        def new_page(a):
            a = a.reshape(dbsz, t_new, groups, SUBLANES, hd_a)
            return jnp.pad(a, ((0, 0), (0, page - t_new), (0, 0), (0, 0), (0, 0)))
        att_samp = _paged_attention(f"s{l}_paged_attn", page_table, lam, qs, ck, cv, new_page(k_s[:n_s]),
                                    new_page(v_s[:n_s]), z_s, sg, layer=l, t_new=t_new,
                                    pages_per_step=8 if page_table.shape[1] % 8 == 0 else 4,
                                    out_scale=out_scale)
        att_meta = _meta_attention(f"s{l}_meta_attn", lam, q_s, k_s, v_s, z_s, sg, heads=h_a, n_meta=n_meta,
                                   row0=n_s, out_scale=out_scale)
        att_s = jnp.concatenate([att_samp, att_meta], axis=0)
        hm_s = jnp.concatenate([hm_samp, hm_meta], axis=0)
        x_s, xn_s = _out_proj(f"s{l}_out", att_s, hm_s, g_s, x_s, wpa, wpm, wo, g_next[l], l, 256, xn_dtype)
        outs["ks"].append(k_s[:n_s].reshape(dbsz, t_new, h_a, hd_a))
        outs["vs"].append(v_s[:n_s].reshape(dbsz, t_new, h_a, hd_a))

        outs["cp"].append(mqk_m.reshape(bsz, seq, 2 * w_qk)[:, -(taps - 1):])
        gi_m, gf_m = _gates(gt_m, b_i[l], b_f[l], h_m, bsz, chunk)
        prev_m = mqk_s[n_s + n_meta - (taps - 1):].reshape(1, taps - 1, 2 * w_qk)
        hm_m, c_p, n_p, m_p = _mlstm(
            f"m{l}_mlstm", mqk_m, prev_m, conv_w[l], conv_b2[l], u5_m, gi_m, gf_m, c_meta, n_meta_s,
            m_meta[:, :, 0, 0].reshape(-1), hg, bsz=bsz, heads=h_m, chunk=chunk, row0=0, out_dtype=bf16)
        outs["Cp"].append(c_p)
        outs["np"].append(n_p.reshape(bsz, h_m, dk_m))
        outs["mp"].append(m_p[:, :, 0, 0])

        km = k_s[n_s:].astype(bf16)
        vmt = v_s[n_s:].T.astype(bf16)
        att_m = _flash_attention(f"m{l}_flash_attn", lam, q_m, k16_m, v16_m, km, vmt, z_m, sg, bsz=bsz, heads=h_a,
                                 tq=tq, hb=4, out_scale=out_scale)
        x_m, xn_m = _out_proj(f"m{l}_out", att_m, hm_m, g_m, x_m, wpa, wpm, wo, g_next[l], l, 256, xn_dtype)

    y_prompt = xn_m.reshape(bsz, seq, d)
    y_sample = xn_s[:n_s].reshape(dbsz, t_new, d)
    st = {k: jnp.stack(v) for k, v in outs.items()}
    kp = kp_buf.reshape(depth, bsz, n_meta + seq, h_a, hd_a)
    vp = vp_buf.reshape(depth, bsz, n_meta + seq, h_a, hd_a)
    return (y_prompt, y_sample, kp, vp, st["ks"], st["vs"], st["cp"], st["cs"],
            st["Cp"], st["np"], st["mp"], st["Cs"], st["ns"], st["ms"])
```
